```python
import math
import jax
import jax.numpy as jnp
from jax import lax
import numpy as np

D_MODEL = 4096
BATCH = 4
SEQ = 2048
DEPTH = 4
DEC_BATCH = 128
DEC_SEQ = 1
PAST_LEN = 8192
PAGE_SIZE = 128

MIXER_ORDER = ('mla', 'diff', 'rwkv7')
N_MIXERS = 3
ROPE_THETA = 500000.0
Q_BLOCK = 128
LN_EPS = 1e-5
DEEPNORM_ALPHA = (2 * DEPTH) ** 0.25
DEEPNORM_BETA = (8 * DEPTH) ** -0.25
ADA_CHUNKS = 6
ADA_SCALE = 0.3

MLA_HEADS = D_MODEL // 128
QK_NOPE = 128
QK_ROPE = 64
V_HEAD = 128
Q_LORA = 1536
KV_LORA = 512
MLA_EPS = 1e-6

DIFF_HEAD = 128
DIFF_HEADS = D_MODEL // (2 * DIFF_HEAD)
DIFF_ROT = DIFF_HEAD // 4
DIFF_EPS = 1e-5

RWKV_HEAD = 64
RWKV_HEADS = D_MODEL // RWKV_HEAD
DECAY_LORA = 128
AAA_LORA = 128
GATE_LORA = 480
RWKV_GN_EPS = 64e-5
N_SHIFT_MIX = 6

D_FF = 11008
N_EXPERTS = 8
TOP_K = 2
D_EXPERT = 3584

f32 = jnp.float32

kernel_name = 'hybrid_mla_diff_rwkv7_moe_decode_step'


def layer_norm(x, g, b):
    xf = x.astype(f32)
    mu = jnp.mean(xf, axis=-1, keepdims=True)
    var = jnp.mean(jnp.square(xf - mu), axis=-1, keepdims=True)
    return ((xf - mu) * lax.rsqrt(var + LN_EPS) * g + b).astype(x.dtype)


def rms_norm(x, g, eps):
    xf = x.astype(f32)
    return (xf * lax.rsqrt(jnp.mean(jnp.square(xf), axis=-1, keepdims=True) + eps) * g).astype(x.dtype)


def apply_rope(x, pos, n_rot):
    half = n_rot // 2
    inv = ROPE_THETA ** (-jnp.arange(half, dtype=f32) / half)
    ang = pos.astype(f32)[:, None] * inv[None, :]
    shape = (1, pos.shape[0]) + (1,) * (x.ndim - 3) + (half,)
    cos, sin = jnp.cos(ang).reshape(shape), jnp.sin(ang).reshape(shape)
    x1 = x[..., :half].astype(f32)
    x2 = x[..., half:n_rot].astype(f32)
    rot = jnp.concatenate([x1 * cos - x2 * sin, x2 * cos + x1 * sin], axis=-1).astype(x.dtype)
    return jnp.concatenate([rot, x[..., n_rot:]], axis=-1)


def masked_softmax(s, mask):
    return jax.nn.softmax(jnp.where(mask, s.astype(f32), -jnp.inf), axis=-1)


def past_new_softmax(s_past, s_new):
    n_new = s_new.shape[-1]
    causal = jnp.tril(jnp.ones((n_new, n_new), dtype=bool))
    s = jnp.concatenate([s_past.astype(f32), jnp.where(causal, s_new.astype(f32), -jnp.inf)], axis=-1)
    p = jax.nn.softmax(s, axis=-1)
    n_past = s_past.shape[-1]
    return p[..., :n_past], p[..., n_past:]


def map_query_blocks(fn, *qs):
    b, s = qs[0].shape[:2]
    nb = s // Q_BLOCK
    blocks = tuple(q.reshape((b, nb, Q_BLOCK) + q.shape[2:]).swapaxes(0, 1) for q in qs)
    starts = jnp.arange(nb, dtype=jnp.int32) * Q_BLOCK
    out = lax.map(lambda a: fn(a[0] + jnp.arange(Q_BLOCK, dtype=jnp.int32), *a[1:]), (starts,) + blocks)
    return out.swapaxes(0, 1).reshape((b, s) + out.shape[3:])


def gather_pages(cache, page_table):
    g = cache[page_table]
    return g.reshape((page_table.shape[0], -1) + cache.shape[2:])


def adaln_terms(c, w, b):
    mod = jax.nn.silu(c) @ w + b
    return [m[:, None, :] for m in jnp.split(mod, ADA_CHUNKS, axis=-1)]


def mla_project(h, pos, p):
    b, s, _ = h.shape
    cq = rms_norm(h @ p['w_dq'], p['q_norm'], MLA_EPS)
    q = (cq @ p['w_uq']).reshape(b, s, MLA_HEADS, QK_NOPE + QK_ROPE)
    q_nope = q[..., :QK_NOPE]
    q_pe = apply_rope(q[..., QK_NOPE:], pos, QK_ROPE)
    kv = h @ p['w_dkv']
    ckv = rms_norm(kv[..., :KV_LORA], p['kv_norm'], MLA_EPS)
    kpe = apply_rope(kv[..., KV_LORA:], pos, QK_ROPE)
    return q_nope, q_pe, ckv, kpe


def mla_prompt(h, p):
    b, s, _ = h.shape
    pos = jnp.arange(s, dtype=jnp.int32)
    q_nope, q_pe, ckv, kpe = mla_project(h, pos, p)
    k_nope = jnp.einsum('bkc,chd->bkhd', ckv, p['w_ukv'][..., :QK_NOPE])
    v = jnp.einsum('bkc,chd->bkhd', ckv, p['w_ukv'][..., QK_NOPE:])
    scale = (QK_NOPE + QK_ROPE) ** -0.5

    def block(qpos, qn, qp):
        sc = (jnp.einsum('bqhd,bkhd->bhqk', qn, k_nope) + jnp.einsum('bqhr,bkr->bhqk', qp, kpe)) * scale
        prob = masked_softmax(sc, qpos[:, None] >= pos[None, :]).astype(v.dtype)
        return jnp.einsum('bhqk,bkhd->bqhd', prob, v)

    o = map_query_blocks(block, q_nope, q_pe)
    return o.reshape(b, s, MLA_HEADS * V_HEAD) @ p['w_o'], ckv, kpe


def mla_sample(h, cache_ckv, cache_kpe, page_table, p):
    b, s, _ = h.shape
    pos = PAST_LEN + jnp.arange(s, dtype=jnp.int32)
    q_nope, q_pe, ckv, kpe = mla_project(h, pos, p)
    w_uk = p['w_ukv'][..., :QK_NOPE]
    w_uv = p['w_ukv'][..., QK_NOPE:]
    q_lat = jnp.einsum('bqhd,chd->bqhc', q_nope, w_uk)
    ckv_past = gather_pages(cache_ckv, page_table)
    kpe_past = gather_pages(cache_kpe, page_table)
    scale = (QK_NOPE + QK_ROPE) ** -0.5
    s_past = (jnp.einsum('bqhc,bkc->bhqk', q_lat, ckv_past) + jnp.einsum('bqhr,bkr->bhqk', q_pe, kpe_past)) * scale
    s_new = (jnp.einsum('bqhc,bkc->bhqk', q_lat, ckv) + jnp.einsum('bqhr,bkr->bhqk', q_pe, kpe)) * scale
    p_past, p_new = past_new_softmax(s_past, s_new)
    o_lat = (jnp.einsum('bhqk,bkc->bqhc', p_past.astype(ckv.dtype), ckv_past)
             + jnp.einsum('bhqk,bkc->bqhc', p_new.astype(ckv.dtype), ckv))
    o = jnp.einsum('bqhc,chd->bqhd', o_lat, w_uv)
    return o.reshape(b, s, MLA_HEADS * V_HEAD) @ p['w_o'], ckv, kpe


def diff_lambda(p, layer_idx):
    lam_init = 0.8 - 0.6 * math.exp(-0.3 * layer_idx)
    lam = (jnp.exp(jnp.sum(p['lam_q1'].astype(f32) * p['lam_k1'].astype(f32)))
           - jnp.exp(jnp.sum(p['lam_q2'].astype(f32) * p['lam_k2'].astype(f32))) + lam_init)
    return lam, lam_init


def diff_project(h, pos, p):
    b, s, _ = h.shape
    q = apply_rope((h @ p['w_q']).reshape(b, s, DIFF_HEADS, 2, DIFF_HEAD), pos, DIFF_ROT)
    k = apply_rope((h @ p['w_k']).reshape(b, s, 2, DIFF_HEAD), pos, DIFF_ROT)
    v = h @ p['w_v']
    return q, k, v


def diff_output(o, lam_init, p):
    b, s = o.shape[:2]
    o = rms_norm(o, p['subln'], DIFF_EPS) * (1.0 - lam_init)
    return o.reshape(b, s, DIFF_HEADS * 2 * DIFF_HEAD) @ p['w_o']


def diff_prompt(h, p, layer_idx):
    b, s, _ = h.shape
    pos = jnp.arange(s, dtype=jnp.int32)
    q, k, v = diff_project(h, pos, p)
    lam, lam_init = diff_lambda(p, layer_idx)
    scale = DIFF_HEAD ** -0.5

    def block(qpos, qb):
        sc = jnp.einsum('bqhcd,bkcd->bchqk', qb, k) * scale
        prob = masked_softmax(sc, qpos[:, None] >= pos[None, :])
        w = (prob[:, 0] - lam * prob[:, 1]).astype(v.dtype)
        return jnp.einsum('bhqk,bkv->bqhv', w, v)

    o = map_query_blocks(block, q)
    return diff_output(o, lam_init, p), k, v


def diff_sample(h, cache_k, cache_v, page_table, p, layer_idx):
    b, s, _ = h.shape
    pos = PAST_LEN + jnp.arange(s, dtype=jnp.int32)
    q, k, v = diff_project(h, pos, p)
    lam, lam_init = diff_lambda(p, layer_idx)
    scale = DIFF_HEAD ** -0.5
    k_past = gather_pages(cache_k, page_table)
    v_past = gather_pages(cache_v, page_table)
    s_past = jnp.einsum('bqhcd,bkcd->bchqk', q, k_past) * scale
    s_new = jnp.einsum('bqhcd,bkcd->bchqk', q, k) * scale
    p_past, p_new = past_new_softmax(s_past, s_new)
    w_past = (p_past[:, 0] - lam * p_past[:, 1]).astype(v.dtype)
    w_new = (p_new[:, 0] - lam * p_new[:, 1]).astype(v.dtype)
    o = jnp.einsum('bhqk,bkv->bqhv', w_past, v_past) + jnp.einsum('bhqk,bkv->bqhv', w_new, v)
    return diff_output(o, lam_init, p), k, v


def wkv7_scan(r, w, k, v, aa, bb, s0):
    tm = lambda t: jnp.moveaxis(t.astype(f32), 1, 0)

    def step(S, xs):
        rt, wt, kt, vt, at, bt = xs
        sa = jnp.einsum('bhij,bhj->bhi', S, at)
        S = S * wt[:, :, None, :] + sa[..., None] * bt[:, :, None, :] + vt[..., None] * kt[:, :, None, :]
        return S, jnp.einsum('bhij,bhj->bhi', S, rt)

    S, y = lax.scan(step, s0.astype(f32), (tm(r), tm(w), tm(k), tm(v), tm(aa), tm(bb)))
    return jnp.moveaxis(y, 0, 1).astype(r.dtype), S


def head_group_norm(y, g, b):
    bsz, s = y.shape[:2]
    yf = y.astype(f32)
    mu = jnp.mean(yf, axis=-1, keepdims=True)
    var = jnp.mean(jnp.square(yf - mu), axis=-1, keepdims=True)
    yn = ((yf - mu) * lax.rsqrt(var + RWKV_GN_EPS)).reshape(bsz, s, D_MODEL)
    return (yn * g + b).astype(y.dtype)


def rwkv7_mix(h, shift0, wkv0, p):
    b, s, d = h.shape
    heads = lambda t: t.reshape(b, s, RWKV_HEADS, RWKV_HEAD)
    prev = jnp.concatenate([shift0[:, None, :].astype(h.dtype), h[:, :-1]], axis=1)
    xx = prev - h
    xr, xw, xk, xv, xa, xg = (h + xx * p['mix'][j] for j in range(N_SHIFT_MIX))
    r = xr @ p['w_r']
    k = xk @ p['w_k']
    v = xv @ p['w_v']
    log_w = -jax.nn.softplus(-(p['w0'] + jnp.tanh(xw @ p['w1']) @ p['w2']).astype(f32)) - 0.5
    decay = jnp.exp(-jnp.exp(log_w))
    a = jax.nn.sigmoid((p['a0'] + (xa @ p['a1']) @ p['a2']).astype(f32))
    g = jax.nn.sigmoid(xg @ p['g1']) @ p['g2']
    kk = heads((k * p['k_k']).astype(f32))
    kk = kk / jnp.maximum(jnp.sqrt(jnp.sum(jnp.square(kk), axis=-1, keepdims=True)), 1e-12)
    k = k * (1.0 + (a.astype(k.dtype) - 1.0) * p['k_a'])
    y, wkv = wkv7_scan(heads(r), heads(decay), heads(k), heads(v), -kk, kk * heads(a), wkv0)
    y = head_group_norm(y, p['lnx_g'], p['lnx_b'])
    y = y + (jnp.sum(heads(r * k) * p['r_k'], axis=-1, keepdims=True) * heads(v)).reshape(b, s, d)
    return (y * g) @ p['w_o'], h[:, -1], wkv


def swiglu(x, w1, w3, w2):
    return (jax.nn.silu(x @ w1) * (x @ w3)) @ w2


def moe_swiglu(x, router, w1, w3, w2):
    b, s, d = x.shape
    t = x.reshape(b * s, d)
    logits = (t @ router).astype(f32)
    top_v, top_i = lax.top_k(logits, TOP_K)
    gates = jax.nn.softmax(top_v, axis=-1)
    dense_gate = jnp.einsum('tk,tke->te', gates, jax.nn.one_hot(top_i, N_EXPERTS, dtype=f32)).astype(x.dtype)
    out = jnp.zeros_like(t)
    for e in range(N_EXPERTS):
        out = out + dense_gate[:, e:e + 1] * swiglu(t, w1[e], w3[e], w2[e])
    return out.reshape(b, s, d)


def setup_inputs(seed: int = 0) -> dict:
    keys = iter(jax.random.split(jax.random.key(seed), 256))

    def normal(shape, scale=1.0):
        return scale * jax.random.normal(next(keys), shape, f32)

    def uniform(shape, lo, hi):
        return jax.random.uniform(next(keys), shape, f32, lo, hi)

    def gain(n):
        return 1.0 + normal((n,), 0.02)

    n_pages = PAST_LEN // PAGE_SIZE
    n_used = DEC_BATCH * n_pages
    n_pool = n_used + (n_used + 3) // 4
    d_in = D_MODEL ** -0.5
    inp = {}
    inp['x_prompt'] = normal((BATCH, SEQ, D_MODEL))
    inp['x_sample'] = normal((DEC_BATCH, DEC_SEQ, D_MODEL))
    inp['cache_l0_ckv'] = normal((n_pool, PAGE_SIZE, KV_LORA))
    inp['cache_l0_kpe'] = normal((n_pool, PAGE_SIZE, QK_ROPE))
    inp['cache_l1_k'] = normal((n_pool, PAGE_SIZE, 2, DIFF_HEAD))
    inp['cache_l1_v'] = normal((n_pool, PAGE_SIZE, 2 * DIFF_HEAD))
    inp['state_l2_shift'] = normal((DEC_BATCH, D_MODEL))
    inp['state_l2_wkv'] = normal((DEC_BATCH, RWKV_HEADS, RWKV_HEAD, RWKV_HEAD), 0.5)
    inp['cache_l3_ckv'] = normal((n_pool, PAGE_SIZE, KV_LORA))
    inp['cache_l3_kpe'] = normal((n_pool, PAGE_SIZE, QK_ROPE))
    inp['page_table'] = jax.random.permutation(next(keys), n_pool)[:n_used].reshape(DEC_BATCH, n_pages).astype(jnp.int32)
    inp['c_prompt'] = normal((BATCH, D_MODEL))
    inp['c_sample'] = normal((DEC_BATCH, D_MODEL))
    for i in range(DEPTH):
        p = 'l%d_' % i
        inp[p + 'ada_w'] = normal((D_MODEL, ADA_CHUNKS * D_MODEL), ADA_SCALE * d_in)
        inp[p + 'ada_b'] = normal((ADA_CHUNKS * D_MODEL,), 0.01)
        inp[p + 'ln1_g'] = gain(D_MODEL)
        inp[p + 'ln1_b'] = normal((D_MODEL,), 0.01)
        inp[p + 'ln2_g'] = gain(D_MODEL)
        inp[p + 'ln2_b'] = normal((D_MODEL,), 0.01)
        kind = MIXER_ORDER[i % N_MIXERS]
        if kind == 'mla':
            inp[p + 'w_dq'] = normal((D_MODEL, Q_LORA), d_in)
            inp[p + 'q_norm'] = gain(Q_LORA)
            inp[p + 'w_uq'] = normal((Q_LORA, MLA_HEADS * (QK_NOPE + QK_ROPE)), Q_LORA ** -0.5)
            inp[p + 'w_dkv'] = normal((D_MODEL, KV_LORA + QK_ROPE), d_in)
            inp[p + 'kv_norm'] = gain(KV_LORA)
            inp[p + 'w_ukv'] = normal((KV_LORA, MLA_HEADS, QK_NOPE + V_HEAD), KV_LORA ** -0.5)
            inp[p + 'w_o'] = normal((MLA_HEADS * V_HEAD, D_MODEL), DEEPNORM_BETA * (MLA_HEADS * V_HEAD) ** -0.5)
        elif kind == 'diff':
            inp[p + 'w_q'] = normal((D_MODEL, DIFF_HEADS * 2 * DIFF_HEAD), d_in)
            inp[p + 'w_k'] = normal((D_MODEL, 2 * DIFF_HEAD), d_in)
            inp[p + 'w_v'] = normal((D_MODEL, 2 * DIFF_HEAD), d_in)
            inp[p + 'lam_q1'] = normal((DIFF_HEAD,), 0.1)
            inp[p + 'lam_k1'] = normal((DIFF_HEAD,), 0.1)
            inp[p + 'lam_q2'] = normal((DIFF_HEAD,), 0.1)
            inp[p + 'lam_k2'] = normal((DIFF_HEAD,), 0.1)
            inp[p + 'subln'] = gain(2 * DIFF_HEAD)
            inp[p + 'w_o'] = normal((DIFF_HEADS * 2 * DIFF_HEAD, D_MODEL), DEEPNORM_BETA * (DIFF_HEADS * 2 * DIFF_HEAD) ** -0.5)
        else:
            inp[p + 'mix'] = uniform((N_SHIFT_MIX, D_MODEL), 0.0, 1.0)
            inp[p + 'w_r'] = normal((D_MODEL, D_MODEL), d_in)
            inp[p + 'w_k'] = normal((D_MODEL, D_MODEL), d_in)
            inp[p + 'w_v'] = normal((D_MODEL, D_MODEL), d_in)
            inp[p + 'w_o'] = normal((D_MODEL, D_MODEL), DEEPNORM_BETA * d_in)
            inp[p + 'w0'] = uniform((D_MODEL,), -6.5, -1.5)
            inp[p + 'w1'] = normal((D_MODEL, DECAY_LORA), d_in)
            inp[p + 'w2'] = normal((DECAY_LORA, D_MODEL), 0.1 * DECAY_LORA ** -0.5)
            inp[p + 'a0'] = normal((D_MODEL,), 0.1)
            inp[p + 'a1'] = normal((D_MODEL, AAA_LORA), d_in)
            inp[p + 'a2'] = normal((AAA_LORA, D_MODEL), 0.5 * AAA_LORA ** -0.5)
            inp[p + 'g1'] = normal((D_MODEL, GATE_LORA), d_in)
            inp[p + 'g2'] = normal((GATE_LORA, D_MODEL), GATE_LORA ** -0.5)
            inp[p + 'k_k'] = 0.85 + normal((D_MODEL,), 0.05)
            inp[p + 'k_a'] = 1.0 + normal((D_MODEL,), 0.05)
            inp[p + 'r_k'] = normal((RWKV_HEADS, RWKV_HEAD), 0.1)
            inp[p + 'lnx_g'] = gain(D_MODEL)
            inp[p + 'lnx_b'] = normal((D_MODEL,), 0.01)
        if i % 2 == 0:
            inp[p + 'ffn_w1'] = normal((D_MODEL, D_FF), d_in)
            inp[p + 'ffn_w3'] = normal((D_MODEL, D_FF), d_in)
            inp[p + 'ffn_w2'] = normal((D_FF, D_MODEL), DEEPNORM_BETA * D_FF ** -0.5)
        else:
            inp[p + 'router'] = normal((D_MODEL, N_EXPERTS), d_in)
            inp[p + 'exp_w1'] = normal((N_EXPERTS, D_MODEL, D_EXPERT), d_in)
            inp[p + 'exp_w3'] = normal((N_EXPERTS, D_MODEL, D_EXPERT), d_in)
            inp[p + 'exp_w2'] = normal((N_EXPERTS, D_EXPERT, D_MODEL), DEEPNORM_BETA * D_EXPERT ** -0.5)
    return inp


def reference(x_prompt, x_sample,
              cache_l0_ckv, cache_l0_kpe, cache_l1_k, cache_l1_v, state_l2_shift, state_l2_wkv,
              cache_l3_ckv, cache_l3_kpe, page_table, c_prompt, c_sample,
              l0_ada_w, l0_ada_b, l0_ln1_g, l0_ln1_b, l0_ln2_g, l0_ln2_b,
              l0_w_dq, l0_q_norm, l0_w_uq, l0_w_dkv, l0_kv_norm, l0_w_ukv, l0_w_o,
              l0_ffn_w1, l0_ffn_w3, l0_ffn_w2,
              l1_ada_w, l1_ada_b, l1_ln1_g, l1_ln1_b, l1_ln2_g, l1_ln2_b,
              l1_w_q, l1_w_k, l1_w_v, l1_lam_q1, l1_lam_k1, l1_lam_q2, l1_lam_k2, l1_subln, l1_w_o,
              l1_router, l1_exp_w1, l1_exp_w3, l1_exp_w2,
              l2_ada_w, l2_ada_b, l2_ln1_g, l2_ln1_b, l2_ln2_g, l2_ln2_b,
              l2_mix, l2_w_r, l2_w_k, l2_w_v, l2_w_o, l2_w0, l2_w1, l2_w2, l2_a0, l2_a1, l2_a2,
              l2_g1, l2_g2, l2_k_k, l2_k_a, l2_r_k, l2_lnx_g, l2_lnx_b,
              l2_ffn_w1, l2_ffn_w3, l2_ffn_w2,
              l3_ada_w, l3_ada_b, l3_ln1_g, l3_ln1_b, l3_ln2_g, l3_ln2_b,
              l3_w_dq, l3_q_norm, l3_w_uq, l3_w_dkv, l3_kv_norm, l3_w_ukv, l3_w_o,
              l3_router, l3_exp_w1, l3_exp_w3, l3_exp_w2):
    layers = (
        dict(ada=(l0_ada_w, l0_ada_b), ln=(l0_ln1_g, l0_ln1_b, l0_ln2_g, l0_ln2_b),
             mix=dict(w_dq=l0_w_dq, q_norm=l0_q_norm, w_uq=l0_w_uq, w_dkv=l0_w_dkv, kv_norm=l0_kv_norm,
                      w_ukv=l0_w_ukv, w_o=l0_w_o),
             state=(cache_l0_ckv, cache_l0_kpe), ffn=(l0_ffn_w1, l0_ffn_w3, l0_ffn_w2)),
        dict(ada=(l1_ada_w, l1_ada_b), ln=(l1_ln1_g, l1_ln1_b, l1_ln2_g, l1_ln2_b),
             mix=dict(w_q=l1_w_q, w_k=l1_w_k, w_v=l1_w_v, lam_q1=l1_lam_q1, lam_k1=l1_lam_k1,
                      lam_q2=l1_lam_q2, lam_k2=l1_lam_k2, subln=l1_subln, w_o=l1_w_o),
             state=(cache_l1_k, cache_l1_v), ffn=(l1_router, l1_exp_w1, l1_exp_w3, l1_exp_w2)),
        dict(ada=(l2_ada_w, l2_ada_b), ln=(l2_ln1_g, l2_ln1_b, l2_ln2_g, l2_ln2_b),
             mix=dict(mix=l2_mix, w_r=l2_w_r, w_k=l2_w_k, w_v=l2_w_v, w_o=l2_w_o, w0=l2_w0, w1=l2_w1, w2=l2_w2,
                      a0=l2_a0, a1=l2_a1, a2=l2_a2, g1=l2_g1, g2=l2_g2, k_k=l2_k_k, k_a=l2_k_a, r_k=l2_r_k,
                      lnx_g=l2_lnx_g, lnx_b=l2_lnx_b),
             state=(state_l2_shift, state_l2_wkv), ffn=(l2_ffn_w1, l2_ffn_w3, l2_ffn_w2)),
        dict(ada=(l3_ada_w, l3_ada_b), ln=(l3_ln1_g, l3_ln1_b, l3_ln2_g, l3_ln2_b),
             mix=dict(w_dq=l3_w_dq, q_norm=l3_q_norm, w_uq=l3_w_uq, w_dkv=l3_w_dkv, kv_norm=l3_kv_norm,
                      w_ukv=l3_w_ukv, w_o=l3_w_o),
             state=(cache_l3_ckv, cache_l3_kpe), ffn=(l3_router, l3_exp_w1, l3_exp_w3, l3_exp_w2)),
    )
    new = {}
    xp, xs = x_prompt, x_sample
    for i in range(DEPTH):
        L = layers[i]
        kind = MIXER_ORDER[i % N_MIXERS]
        mp = adaln_terms(c_prompt, *L['ada'])
        ms = adaln_terms(c_sample, *L['ada'])
        hp = xp * (1.0 + mp[1]) + mp[0]
        hs = xs * (1.0 + ms[1]) + ms[0]
        if kind == 'mla':
            op, a_p, b_p = mla_prompt(hp, L['mix'])
            os_, a_s, b_s = mla_sample(hs, L['state'][0], L['state'][1], page_table, L['mix'])
        elif kind == 'diff':
            op, a_p, b_p = diff_prompt(hp, L['mix'], i)
            os_, a_s, b_s = diff_sample(hs, L['state'][0], L['state'][1], page_table, L['mix'], i)
        else:
            zero_shift = jnp.zeros((hp.shape[0], D_MODEL), hp.dtype)
            zero_wkv = jnp.zeros((hp.shape[0], RWKV_HEADS, RWKV_HEAD, RWKV_HEAD), f32)
            op, a_p, b_p = rwkv7_mix(hp, zero_shift, zero_wkv, L['mix'])
            os_, a_s, b_s = rwkv7_mix(hs, L['state'][0], L['state'][1], L['mix'])
        new['l%d_prompt' % i] = (a_p, b_p)
        new['l%d_sample' % i] = (a_s, b_s)
        g1, bb1, g2, bb2 = L['ln']
        xp = layer_norm(DEEPNORM_ALPHA * xp + (1.0 + mp[2]) * op, g1, bb1)
        xs = layer_norm(DEEPNORM_ALPHA * xs + (1.0 + ms[2]) * os_, g1, bb1)
        hp = xp * (1.0 + mp[4]) + mp[3]
        hs = xs * (1.0 + ms[4]) + ms[3]
        if i % 2 == 0:
            fp = swiglu(hp, *L['ffn'])
            fs = swiglu(hs, *L['ffn'])
        else:
            fp = moe_swiglu(hp, *L['ffn'])
            fs = moe_swiglu(hs, *L['ffn'])
        xp = layer_norm(DEEPNORM_ALPHA * xp + (1.0 + mp[5]) * fp, g2, bb2)
        xs = layer_norm(DEEPNORM_ALPHA * xs + (1.0 + ms[5]) * fs, g2, bb2)
    return (xp, xs,
            new['l0_prompt'][0], new['l0_prompt'][1], new['l0_sample'][0], new['l0_sample'][1],
            new['l1_prompt'][0], new['l1_prompt'][1], new['l1_sample'][0], new['l1_sample'][1],
            new['l2_prompt'][0], new['l2_prompt'][1], new['l2_sample'][0], new['l2_sample'][1],
            new['l3_prompt'][0], new['l3_prompt'][1], new['l3_sample'][0], new['l3_sample'][1])
```

```python
import functools
import math

import jax
import jax.numpy as jnp
from jax import lax
from jax.experimental import pallas as pl
from jax.experimental.pallas import tpu as pltpu

F32 = jnp.float32
BF16 = jnp.bfloat16

LANES = 128
VMEM_LIMIT_BYTES = 56 * 1024 * 1024

ROPE_THETA = 500000.0
LN_EPS = 1e-5
MLA_EPS = 1e-6
DIFF_EPS = 1e-5
RWKV_GN_EPS = 64e-5
QK_NOPE = 128
QK_ROPE = 64
V_HEAD = 128
DIFF_HEAD = 128
DIFF_ROT = DIFF_HEAD // 4
RWKV_HEAD = 64
TOP_K = 2
PAGE = 128


def _params(*sem):
    return pltpu.CompilerParams(dimension_semantics=sem, vmem_limit_bytes=VMEM_LIMIT_BYTES)


def _row_tile(m, cap):
    if m <= cap:
        return m
    for t in range(cap - cap % 16, 15, -16):
        if m % t == 0:
            return t
    return cap


def _act(v, act):
    if act == "tanh":
        return jnp.tanh(v)
    if act == "sigmoid":
        return jax.nn.sigmoid(v)
    return v


def _matmul_kernel(*refs, nk, has_bias, act):
    if has_bias:
        x_ref, w_ref, b_ref, o_ref = refs
    else:
        x_ref, w_ref, o_ref = refs
        b_ref = None
    acc = jnp.dot(x_ref[...].astype(BF16), w_ref[...].astype(BF16), preferred_element_type=F32)

    def finish(v):
        if has_bias:
            v = v + b_ref[...]
        return _act(v, act).astype(o_ref.dtype)

    if nk == 1:
        o_ref[...] = finish(acc)
    else:
        k = pl.program_id(2)

        @pl.when(k == 0)
        def _():
            o_ref[...] = acc

        @pl.when(jnp.logical_and(k > 0, k < nk - 1))
        def _():
            o_ref[...] += acc

        @pl.when(k == nk - 1)
        def _():
            o_ref[...] = finish(o_ref[...] + acc)


def _k_split(k):
    if k <= 4096:
        return 1
    for nk in range(2, 65):
        if k % nk == 0 and (k // nk) % LANES == 0 and k // nk <= 5504:
            return nk
    raise ValueError(f"no K split for {k}")


def matmul(x, w, bias=None, act=None, out_dtype=F32, tm_cap=1024):
    m, k = x.shape
    n = w.shape[1]
    nk = _k_split(k)
    tk = k // nk
    tm = _row_tile(m, tm_cap)
    if nk > 1:
        tn = 256
        assert out_dtype == F32
    else:
        tn = n if n <= 640 else 512
    grid = (pl.cdiv(m, tm), pl.cdiv(n, tn), nk)
    in_specs = [pl.BlockSpec((tm, tk), lambda i, j, kk: (i, kk)),
                pl.BlockSpec((tk, tn), lambda i, j, kk: (kk, j))]
    args = [x, w]
    if bias is not None:
        in_specs.append(pl.BlockSpec((1, tn), lambda i, j, kk: (0, j)))
        args.append(bias.reshape(1, n).astype(F32))
    return pl.pallas_call(
        functools.partial(_matmul_kernel, nk=nk, has_bias=bias is not None, act=act),
        grid=grid, in_specs=in_specs,
        out_specs=pl.BlockSpec((tm, tn), lambda i, j, kk: (i, j)),
        out_shape=jax.ShapeDtypeStruct((m, n), out_dtype),
        compiler_params=_params("parallel", "parallel", "arbitrary"),
    )(*args)


def _swiglu_kernel(x_ref, w1_ref, w3_ref, o_ref):
    x = x_ref[...].astype(BF16)
    a = jnp.dot(x, w1_ref[...].astype(BF16), preferred_element_type=F32)
    b = jnp.dot(x, w3_ref[...].astype(BF16), preferred_element_type=F32)
    o_ref[...] = (a * jax.nn.sigmoid(a) * b).astype(o_ref.dtype)


def swiglu_up(x, w1, w3, tm_cap=1024, tn=256):
    m, k = x.shape
    n = w1.shape[1]
    tm = _row_tile(m, tm_cap)
    return pl.pallas_call(
        _swiglu_kernel,
        grid=(pl.cdiv(m, tm), pl.cdiv(n, tn)),
        in_specs=[pl.BlockSpec((tm, k), lambda i, j: (i, 0)),
                  pl.BlockSpec((k, tn), lambda i, j: (0, j)),
                  pl.BlockSpec((k, tn), lambda i, j: (0, j))],
        out_specs=pl.BlockSpec((tm, tn), lambda i, j: (i, j)),
        out_shape=jax.ShapeDtypeStruct((m, n), BF16),
        compiler_params=_params("parallel", "parallel"),
    )(x, w1, w3)


def _grouped_swiglu_kernel(te_ref, nu_ref, x_ref, w1_ref, w3_ref, o_ref):
    i = pl.program_id(1)

    @pl.when(i < nu_ref[0])
    def _():
        x = x_ref[...]
        a = jnp.dot(x, w1_ref[...].astype(BF16), preferred_element_type=F32)
        b = jnp.dot(x, w3_ref[...].astype(BF16), preferred_element_type=F32)
        o_ref[...] = (a * jax.nn.sigmoid(a) * b).astype(o_ref.dtype)

    @pl.when(i >= nu_ref[0])
    def _():
        o_ref[...] = jnp.zeros_like(o_ref)


def grouped_swiglu_up(tile_expert, n_used, xg, w1, w3, tm, tn=256):
    p, k = xg.shape
    n = w1.shape[2]
    grid_spec = pltpu.PrefetchScalarGridSpec(
        num_scalar_prefetch=2, grid=(n // tn, p // tm),
        in_specs=[pl.BlockSpec((tm, k), lambda j, i, te, nu: (i, 0)),
                  pl.BlockSpec((None, k, tn), lambda j, i, te, nu: (te[i], 0, j)),
                  pl.BlockSpec((None, k, tn), lambda j, i, te, nu: (te[i], 0, j))],
        out_specs=pl.BlockSpec((tm, tn), lambda j, i, te, nu: (i, j)))
    return pl.pallas_call(
        _grouped_swiglu_kernel, grid_spec=grid_spec,
        out_shape=jax.ShapeDtypeStruct((p, n), BF16),
        compiler_params=_params("parallel", "arbitrary"),
    )(tile_expert, n_used, xg, w1, w3)


def _grouped_down_kernel(te_ref, nu_ref, h_ref, w_ref, g_ref, o_ref):
    i = pl.program_id(1)

    @pl.when(i < nu_ref[0])
    def _():
        y = jnp.dot(h_ref[...], w_ref[...].astype(BF16), preferred_element_type=F32)
        o_ref[...] = y * g_ref[...]

    @pl.when(i >= nu_ref[0])
    def _():
        o_ref[...] = jnp.zeros_like(o_ref)


def grouped_down(tile_expert, n_used, h, w2, gate_col, tm, tn=512):
    p, k = h.shape
    n = w2.shape[2]
    grid_spec = pltpu.PrefetchScalarGridSpec(
        num_scalar_prefetch=2, grid=(n // tn, p // tm),
        in_specs=[pl.BlockSpec((tm, k), lambda j, i, te, nu: (i, 0)),
                  pl.BlockSpec((None, k, tn), lambda j, i, te, nu: (te[i], 0, j)),
                  pl.BlockSpec((tm, 1), lambda j, i, te, nu: (i, 0))],
        out_specs=pl.BlockSpec((tm, tn), lambda j, i, te, nu: (i, j)))
    return pl.pallas_call(
        _grouped_down_kernel, grid_spec=grid_spec,
        out_shape=jax.ShapeDtypeStruct((p, n), F32),
        compiler_params=_params("parallel", "arbitrary"),
    )(tile_expert, n_used, h, w2, gate_col)


def _gather_rows_kernel(idx_ref, x_hbm, o_ref, sem, *, rows):
    base = pl.program_id(0) * rows

    def row_copy(r, src):
        return pltpu.make_async_copy(x_hbm.at[pl.ds(src, 1)], o_ref.at[pl.ds(r, 1)], sem)

    def issue(r, carry):
        row_copy(r, idx_ref[base + r]).start()
        return carry

    def drain(r, carry):
        row_copy(r, 0).wait()
        return carry

    lax.fori_loop(0, rows, issue, 0)
    lax.fori_loop(0, rows, drain, 0)


def gather_rows(idx, x, rows=128):
    p = idx.shape[0]
    c = x.shape[1]
    rows = _row_tile(p, rows)
    grid_spec = pltpu.PrefetchScalarGridSpec(
        num_scalar_prefetch=1, grid=(p // rows,),
        in_specs=[pl.BlockSpec(memory_space=pl.ANY)],
        out_specs=pl.BlockSpec((rows, c), lambda i, idx_ref: (i, 0)),
        scratch_shapes=[pltpu.SemaphoreType.DMA(())])
    return pl.pallas_call(
        functools.partial(_gather_rows_kernel, rows=rows), grid_spec=grid_spec,
        out_shape=jax.ShapeDtypeStruct((p, c), x.dtype),
        compiler_params=_params("arbitrary"),
    )(idx, x)


def _combine_kernel(idx_ref, y_hbm, o_ref, buf0, buf1, sem, *, rows):
    base = pl.program_id(0) * rows

    def row_copy(r, src, buf):
        return pltpu.make_async_copy(y_hbm.at[pl.ds(src, 1)], buf.at[pl.ds(r, 1)], sem)

    def issue(r, carry):
        row_copy(r, idx_ref[2 * (base + r)], buf0).start()
        row_copy(r, idx_ref[2 * (base + r) + 1], buf1).start()
        return carry

    def drain(r, carry):
        row_copy(r, 0, buf0).wait()
        row_copy(r, 0, buf1).wait()
        return carry

    lax.fori_loop(0, rows, issue, 0)
    lax.fori_loop(0, rows, drain, 0)
    o_ref[...] = buf0[...] + buf1[...]


def combine_pairs(idx2, y, rows=128):
    t = idx2.shape[0] // 2
    c = y.shape[1]
    rows = _row_tile(t, rows)
    grid_spec = pltpu.PrefetchScalarGridSpec(
        num_scalar_prefetch=1, grid=(t // rows,),
        in_specs=[pl.BlockSpec(memory_space=pl.ANY)],
        out_specs=pl.BlockSpec((rows, c), lambda i, idx_ref: (i, 0)),
        scratch_shapes=[pltpu.VMEM((rows, c), F32), pltpu.VMEM((rows, c), F32),
                        pltpu.SemaphoreType.DMA(())])
    return pl.pallas_call(
        functools.partial(_combine_kernel, rows=rows), grid_spec=grid_spec,
        out_shape=jax.ShapeDtypeStruct((t, c), F32),
        compiler_params=_params("arbitrary"),
    )(idx2, y)


def _norm_mod_kernel(*refs, alpha, do_norm, do_mod, n_prompt_tiles, rows_per_batch_tiles):
    it = iter(refs)
    x_ref = next(it)
    is_sample = pl.program_id(0) >= n_prompt_tiles

    def pick(p_ref, s_ref):
        return jnp.where(is_sample, s_ref[...], p_ref[0])

    if do_norm:
        sub_ref, gp_ref, gs_ref, g_ref, b_ref = (next(it) for _ in range(5))
    if do_mod:
        scp_ref, scs_ref, shp_ref, shs_ref = (next(it) for _ in range(4))
    outs = list(it)
    x = x_ref[...]
    if do_norm:
        z = alpha * x + (1.0 + pick(gp_ref, gs_ref)) * sub_ref[...]
        mu = jnp.mean(z, axis=-1, keepdims=True)
        zc = z - mu
        var = jnp.mean(zc * zc, axis=-1, keepdims=True)
        x = zc * lax.rsqrt(var + LN_EPS) * g_ref[...] + b_ref[...]
        outs.pop(0)[...] = x
    if do_mod:
        h = x * (1.0 + pick(scp_ref, scs_ref)) + pick(shp_ref, shs_ref)
        for o_ref in outs:
            o_ref[...] = h.astype(o_ref.dtype)


def norm_mod(x, *, n_prompt, seq, sub=None, gate=None, ln=None, scale=None, shift=None,
             h_dtypes=(BF16,), alpha=1.0, rows=128):
    nt, d = x.shape
    ns = nt - n_prompt
    rows = min(rows, ns)
    assert ns % rows == 0 and seq % rows == 0
    n_prompt_tiles = n_prompt // rows
    tiles_per_batch = seq // rows
    nb = n_prompt // seq
    row_spec = pl.BlockSpec((rows, d), lambda i: (i, 0))
    pb_spec = pl.BlockSpec((1, 1, d), lambda i: (jnp.minimum(i // tiles_per_batch, nb - 1), 0, 0))
    ps_spec = pl.BlockSpec((rows, d), lambda i: (jnp.maximum(i - n_prompt_tiles, 0), 0))
    vec_spec = pl.BlockSpec((1, d), lambda i: (0, 0))
    args, in_specs = [x], [row_spec]
    do_norm, do_mod = sub is not None, scale is not None
    out_shape, out_specs = [], []
    if do_norm:
        args += [sub, gate[0], gate[1], ln[0].reshape(1, d), ln[1].reshape(1, d)]
        in_specs += [row_spec, pb_spec, ps_spec, vec_spec, vec_spec]
        out_shape.append(jax.ShapeDtypeStruct((nt, d), F32))
        out_specs.append(row_spec)
    if do_mod:
        args += [scale[0], scale[1], shift[0], shift[1]]
        in_specs += [pb_spec, ps_spec, pb_spec, ps_spec]
        for dt in h_dtypes:
            out_shape.append(jax.ShapeDtypeStruct((nt, d), dt))
            out_specs.append(row_spec)
    return pl.pallas_call(
        functools.partial(_norm_mod_kernel, alpha=alpha, do_norm=do_norm, do_mod=do_mod,
                          n_prompt_tiles=n_prompt_tiles, rows_per_batch_tiles=tiles_per_batch),
        grid=(nt // rows,), in_specs=in_specs, out_specs=out_specs, out_shape=out_shape,
        compiler_params=_params("parallel"),
    )(*args)


def _softmax_update(s, v, m_ref, l_ref, acc_ref, idx):
    m_prev = m_ref[idx]
    m_new = jnp.maximum(m_prev, jnp.max(s, axis=-1, keepdims=True))
    alpha = jnp.exp(m_prev - m_new)
    p = jnp.exp(s - m_new)
    l_ref[idx] = alpha * l_ref[idx] + jnp.sum(p, axis=-1, keepdims=True)
    acc_ref[idx] = alpha * acc_ref[idx] + jnp.dot(p.astype(BF16), v, preferred_element_type=F32)
    m_ref[idx] = m_new


def _causal_mask(s, qi, ki, tq, tk):
    row = qi * tq + lax.broadcasted_iota(jnp.int32, s.shape, 0)
    col = ki * tk + lax.broadcasted_iota(jnp.int32, s.shape, 1)
    return jnp.where(row >= col, s, -jnp.inf)


def _dot_nt(a, b):
    return lax.dot_general(a, b, (((1,), (1,)), ((), ())), preferred_element_type=F32)


def _mla_flash_kernel(qn_ref, qp_ref, kv_ref, kpe_ref, o_ref, m_ref, l_ref, acc_ref, *, scale, tq, tk):
    qi, ki = pl.program_id(2), pl.program_id(3)

    @pl.when(ki == 0)
    def _():
        m_ref[...] = jnp.full_like(m_ref, -jnp.inf)
        l_ref[...] = jnp.zeros_like(l_ref)
        acc_ref[...] = jnp.zeros_like(acc_ref)

    def update(masked):
        qp = qp_ref[0]
        for h in range(2):
            q = jnp.concatenate([qn_ref[0, :, h * QK_NOPE:(h + 1) * QK_NOPE], qp], axis=-1)
            k = jnp.concatenate([kv_ref[0, :, h * 256:h * 256 + QK_NOPE],
                                 kpe_ref[0, :, h * LANES:(h + 1) * LANES]], axis=-1)
            v = kv_ref[0, :, h * 256 + QK_NOPE:(h + 1) * 256]
            s = _dot_nt(q, k) * scale
            if masked:
                s = _causal_mask(s, qi, ki, tq, tk)
            _softmax_update(s, v, m_ref, l_ref, acc_ref, h)

    @pl.when(ki < qi)
    def _():
        update(False)

    @pl.when(ki == qi)
    def _():
        update(True)
        for h in range(2):
            o_ref[0, :, h * V_HEAD:(h + 1) * V_HEAD] = (acc_ref[h] / l_ref[h]).astype(o_ref.dtype)


def mla_flash(qn, qp, kvb, kpe2, scale, tq=512):
    b, s, hd = qn.shape
    hp = hd // (2 * QK_NOPE)
    tq = min(tq, s)
    tk = tq
    kv_idx = lambda bi, h, qi, ki: (bi, jnp.minimum(ki, qi), h)
    return pl.pallas_call(
        functools.partial(_mla_flash_kernel, scale=scale, tq=tq, tk=tk),
        grid=(b, hp, s // tq, s // tk),
        in_specs=[pl.BlockSpec((1, tq, 2 * QK_NOPE), lambda bi, h, qi, ki: (bi, qi, h)),
                  pl.BlockSpec((1, tq, 2 * QK_ROPE), lambda bi, h, qi, ki: (bi, qi, h)),
                  pl.BlockSpec((1, tk, 512), kv_idx),
                  pl.BlockSpec((1, tk, 256), lambda bi, h, qi, ki: (bi, jnp.minimum(ki, qi), 0))],
        out_specs=pl.BlockSpec((1, tq, 2 * V_HEAD), lambda bi, h, qi, ki: (bi, qi, h)),
        out_shape=jax.ShapeDtypeStruct((b, s, hp * 2 * V_HEAD), BF16),
        scratch_shapes=[pltpu.VMEM((2, tq, 1), F32), pltpu.VMEM((2, tq, 1), F32),
                        pltpu.VMEM((2, tq, V_HEAD), F32)],
        compiler_params=_params("parallel", "parallel", "parallel", "arbitrary"),
    )(qn, qp, kvb, kpe2)


def _diff_finish(acc_ref, l_ref, lam_ref, g_ref):
    o = acc_ref[0] / l_ref[0] - lam_ref[...] * (acc_ref[1] / l_ref[1])
    return o * lax.rsqrt(jnp.mean(o * o, axis=-1, keepdims=True) + DIFF_EPS) * g_ref[...]


def _diff_flash_kernel(q_ref, k_ref, v_ref, lam_ref, g_ref, o_ref, m_ref, l_ref, acc_ref, *, scale, tq, tk):
    qi, ki = pl.program_id(2), pl.program_id(3)

    @pl.when(ki == 0)
    def _():
        m_ref[...] = jnp.full_like(m_ref, -jnp.inf)
        l_ref[...] = jnp.zeros_like(l_ref)
        acc_ref[...] = jnp.zeros_like(acc_ref)

    def update(masked):
        v = v_ref[0]
        for c in range(2):
            s = _dot_nt(q_ref[0, :, c * DIFF_HEAD:(c + 1) * DIFF_HEAD],
                        k_ref[0, :, c * DIFF_HEAD:(c + 1) * DIFF_HEAD]) * scale
            if masked:
                s = _causal_mask(s, qi, ki, tq, tk)
            _softmax_update(s, v, m_ref, l_ref, acc_ref, c)

    @pl.when(ki < qi)
    def _():
        update(False)

    @pl.when(ki == qi)
    def _():
        update(True)
        o_ref[0] = _diff_finish(acc_ref, l_ref, lam_ref, g_ref).astype(o_ref.dtype)


def diff_flash(q, k, v, lam_row, gain_row, scale, tq=512):
    b, s, hd = q.shape
    nh = hd // (2 * DIFF_HEAD)
    tq = min(tq, s)
    tk = tq
    kv_idx = lambda bi, h, qi, ki: (bi, jnp.minimum(ki, qi), 0)
    vec = pl.BlockSpec((1, 2 * DIFF_HEAD), lambda bi, h, qi, ki: (0, 0))
    return pl.pallas_call(
        functools.partial(_diff_flash_kernel, scale=scale, tq=tq, tk=tk),
        grid=(b, nh, s // tq, s // tk),
        in_specs=[pl.BlockSpec((1, tq, 2 * DIFF_HEAD), lambda bi, h, qi, ki: (bi, qi, h)),
                  pl.BlockSpec((1, tk, 2 * DIFF_HEAD), kv_idx),
                  pl.BlockSpec((1, tk, 2 * DIFF_HEAD), kv_idx), vec, vec],
        out_specs=pl.BlockSpec((1, tq, 2 * DIFF_HEAD), lambda bi, h, qi, ki: (bi, qi, h)),
        out_shape=jax.ShapeDtypeStruct((b, s, hd), BF16),
        scratch_shapes=[pltpu.VMEM((2, tq, 1), F32), pltpu.VMEM((2, tq, 1), F32),
                        pltpu.VMEM((2, tq, 2 * DIFF_HEAD), F32)],
        compiler_params=_params("parallel", "parallel", "parallel", "arbitrary"),
    )(q, k, v, lam_row, gain_row)


def _mla_decode_kernel(pt_ref, ql_ref, qp_ref, cn_ref, kn_ref, *refs, scale, pg):
    ckv_refs, kpe_refs = refs[:pg], refs[pg:2 * pg]
    o_ref, m_ref, l_ref, acc_ref = refs[2 * pg:]
    c = pl.program_id(1)
    ql, qp = ql_ref[0], qp_ref[0]

    @pl.when(c == 0)
    def _():
        cn = cn_ref[0].astype(BF16).astype(F32)
        kn = kn_ref[0].astype(BF16).astype(F32)
        s_new = (jnp.sum(ql.astype(F32) * cn, axis=-1, keepdims=True)
                 + jnp.sum(qp.astype(F32) * kn, axis=-1, keepdims=True)) * scale
        m_ref[...] = s_new
        l_ref[...] = jnp.ones_like(l_ref)
        acc_ref[...] = jnp.broadcast_to(cn, acc_ref.shape)

    pages = [r[0].astype(BF16) for r in ckv_refs]
    s = jnp.concatenate(
        [_dot_nt(ql, pages[g]) + _dot_nt(qp, kpe_refs[g][0].astype(BF16)) for g in range(pg)],
        axis=-1) * scale
    m_prev = m_ref[...]
    m_new = jnp.maximum(m_prev, jnp.max(s, axis=-1, keepdims=True))
    alpha = jnp.exp(m_prev - m_new)
    p = jnp.exp(s - m_new)
    l_ref[...] = alpha * l_ref[...] + jnp.sum(p, axis=-1, keepdims=True)
    pb = p.astype(BF16)
    acc = alpha * acc_ref[...]
    for g in range(pg):
        acc = acc + jnp.dot(pb[:, g * PAGE:(g + 1) * PAGE], pages[g], preferred_element_type=F32)
    acc_ref[...] = acc
    m_ref[...] = m_new

    @pl.when(c == pl.num_programs(1) - 1)
    def _():
        o_ref[0] = (acc_ref[...] / l_ref[...]).astype(o_ref.dtype)


def mla_decode(page_table, q_lat, q_pe, ckv_new, kpe_new, cache_ckv, cache_kpe, scale, pg=16):
    b, h, c_dim = q_lat.shape
    r_dim = q_pe.shape[2]
    n_pages = page_table.shape[1]
    pg = min(pg, n_pages)
    pt = page_table.reshape(-1)

    def page_idx(g):
        return lambda bi, ci, pt_ref: (pt_ref[bi * n_pages + ci * pg + g], 0, 0)

    per_b = lambda bi, ci, pt_ref: (bi, 0, 0)
    in_specs = [pl.BlockSpec((1, h, c_dim), per_b), pl.BlockSpec((1, h, r_dim), per_b),
                pl.BlockSpec((1, 1, c_dim), per_b), pl.BlockSpec((1, 1, r_dim), per_b)]
    in_specs += [pl.BlockSpec((1, PAGE, c_dim), page_idx(g)) for g in range(pg)]
    in_specs += [pl.BlockSpec((1, PAGE, r_dim), page_idx(g)) for g in range(pg)]
    grid_spec = pltpu.PrefetchScalarGridSpec(
        num_scalar_prefetch=1, grid=(b, n_pages // pg), in_specs=in_specs,
        out_specs=pl.BlockSpec((1, h, c_dim), per_b),
        scratch_shapes=[pltpu.VMEM((h, 1), F32), pltpu.VMEM((h, 1), F32), pltpu.VMEM((h, c_dim), F32)])
    return pl.pallas_call(
        functools.partial(_mla_decode_kernel, scale=scale, pg=pg), grid_spec=grid_spec,
        out_shape=jax.ShapeDtypeStruct((b, h, c_dim), BF16),
        compiler_params=_params("parallel", "arbitrary"),
    )(pt, q_lat, q_pe, ckv_new, kpe_new, *([cache_ckv] * pg), *([cache_kpe] * pg))


def _diff_decode_kernel(pt_ref, q_ref, kn_ref, vn_ref, lam_ref, g_ref, *refs, scale, pg):
    k_refs, v_refs = refs[:pg], refs[pg:2 * pg]
    o_ref, m_ref, l_ref, acc_ref = refs[2 * pg:]
    c = pl.program_id(1)

    @pl.when(c == 0)
    def _():
        kn = kn_ref[0].astype(BF16).astype(F32)
        vn = vn_ref[0].astype(BF16).astype(F32)
        for mp in range(2):
            qm = q_ref[0, mp].astype(F32)
            km = kn[:, mp * DIFF_HEAD:(mp + 1) * DIFF_HEAD]
            m_ref[mp] = jnp.sum(qm * km, axis=-1, keepdims=True) * scale
            acc_ref[mp] = jnp.broadcast_to(vn, acc_ref.shape[1:])
        l_ref[...] = jnp.ones_like(l_ref)

    ks = [r[0].astype(BF16) for r in k_refs]
    vs = [r[0].astype(BF16) for r in v_refs]
    for mp in range(2):
        qm = q_ref[0, mp]
        s = jnp.concatenate(
            [_dot_nt(qm, ks[g][:, mp * DIFF_HEAD:(mp + 1) * DIFF_HEAD]) for g in range(pg)],
            axis=-1) * scale
        m_prev = m_ref[mp]
        m_new = jnp.maximum(m_prev, jnp.max(s, axis=-1, keepdims=True))
        alpha = jnp.exp(m_prev - m_new)
        p = jnp.exp(s - m_new)
        l_ref[mp] = alpha * l_ref[mp] + jnp.sum(p, axis=-1, keepdims=True)
        pb = p.astype(BF16)
        acc = alpha * acc_ref[mp]
        for g in range(pg):
            acc = acc + jnp.dot(pb[:, g * PAGE:(g + 1) * PAGE], vs[g], preferred_element_type=F32)
        acc_ref[mp] = acc
        m_ref[mp] = m_new

    @pl.when(c == pl.num_programs(1) - 1)
    def _():
        o_ref[0] = _diff_finish(acc_ref, l_ref, lam_ref, g_ref).astype(o_ref.dtype)


def diff_decode(page_table, q, k_new, v_new, lam_row, gain_row, cache_k, cache_v, scale, pg=16):
    b, _, h, _ = q.shape
    n_pages = page_table.shape[1]
    pg = min(pg, n_pages)
    pt = page_table.reshape(-1)
    w = 2 * DIFF_HEAD

    def page_idx(g):
        return lambda bi, ci, pt_ref: (pt_ref[bi * n_pages + ci * pg + g], 0, 0)

    per_b3 = lambda bi, ci, pt_ref: (bi, 0, 0)
    vec = pl.BlockSpec((1, w), lambda bi, ci, pt_ref: (0, 0))
    in_specs = [pl.BlockSpec((1, 2, h, DIFF_HEAD), lambda bi, ci, pt_ref: (bi, 0, 0, 0)),
                pl.BlockSpec((1, 1, w), per_b3), pl.BlockSpec((1, 1, w), per_b3), vec, vec]
    in_specs += [pl.BlockSpec((1, PAGE, w), page_idx(g)) for g in range(pg)] * 2
    grid_spec = pltpu.PrefetchScalarGridSpec(
        num_scalar_prefetch=1, grid=(b, n_pages // pg), in_specs=in_specs,
        out_specs=pl.BlockSpec((1, h, w), per_b3),
        scratch_shapes=[pltpu.VMEM((2, h, 1), F32), pltpu.VMEM((2, h, 1), F32), pltpu.VMEM((2, h, w), F32)])
    return pl.pallas_call(
        functools.partial(_diff_decode_kernel, scale=scale, pg=pg), grid_spec=grid_spec,
        out_shape=jax.ShapeDtypeStruct((b, h, w), BF16),
        compiler_params=_params("parallel", "arbitrary"),
    )(pt, q, k_new, v_new, lam_row, gain_row, *([cache_k] * pg), *([cache_v] * pg))


def _head_nt_kernel(x_ref, w_ref, o_ref):
    o_ref[...] = _dot_nt(x_ref[...].astype(BF16), w_ref[...].astype(BF16)).astype(o_ref.dtype)


def _head_nn_kernel(x_ref, w_ref, o_ref):
    o_ref[...] = jnp.dot(x_ref[...].astype(BF16), w_ref[...].astype(BF16),
                         preferred_element_type=F32).astype(o_ref.dtype)


def absorb_q(q_nope, w_ukv2):
    b, hd = q_nope.shape
    c_dim = w_ukv2.shape[0]
    nh = hd // QK_NOPE
    return pl.pallas_call(
        _head_nt_kernel, grid=(nh,),
        in_specs=[pl.BlockSpec((b, QK_NOPE), lambda h: (0, h)),
                  pl.BlockSpec((c_dim, QK_NOPE), lambda h: (0, 2 * h))],
        out_specs=pl.BlockSpec((b, c_dim), lambda h: (0, h)),
        out_shape=jax.ShapeDtypeStruct((b, nh * c_dim), BF16),
        compiler_params=_params("parallel"),
    )(q_nope, w_ukv2)


def expand_o(o_lat, w_ukv2):
    b = o_lat.shape[0]
    c_dim = w_ukv2.shape[0]
    nh = o_lat.shape[1] // c_dim
    return pl.pallas_call(
        _head_nn_kernel, grid=(nh,),
        in_specs=[pl.BlockSpec((b, c_dim), lambda h: (0, h)),
                  pl.BlockSpec((c_dim, V_HEAD), lambda h: (0, 2 * h + 1))],
        out_specs=pl.BlockSpec((b, V_HEAD), lambda h: (0, h)),
        out_shape=jax.ShapeDtypeStruct((b, nh * V_HEAD), BF16),
        compiler_params=_params("parallel"),
    )(o_lat, w_ukv2)


def _wkv_kernel(r_ref, w_ref, k_ref, v_ref, a_ref, b_ref, s0_ref, y_ref, st_ref, s_ref, *, ts, n):
    c = pl.program_id(1)

    @pl.when(c == 0)
    def _():
        s_ref[...] = s0_ref[...]

    def row(ref, t, j):
        return ref[t, pl.ds(j, 1), :]

    sa0 = jnp.zeros((n, LANES), F32)
    for j in range(n):
        sa0 = sa0 + s_ref[j] * row(a_ref, 0, j)

    def step(t, sa):
        t_next = jnp.minimum(t + 1, ts - 1)
        v = v_ref[t]
        y = jnp.zeros((n, LANES), F32)
        sa_next = jnp.zeros((n, LANES), F32)
        for j in range(n):
            sj = s_ref[j] * row(w_ref, t, j) + sa * row(b_ref, t, j) + v * row(k_ref, t, j)
            s_ref[j] = sj
            y = y + sj * row(r_ref, t, j)
            sa_next = sa_next + sj * row(a_ref, t_next, j)
        y_ref[t] = y
        return sa_next

    lax.fori_loop(0, ts, step, sa0)

    @pl.when(c == pl.num_programs(1) - 1)
    def _():
        st_ref[...] = s_ref[...]


def wkv_scan(r, w, k, v, a, b, s0, ts=32):
    s, n, l = r.shape
    ts = min(ts, s)
    seq_spec = pl.BlockSpec((ts, n, LANES), lambda g, c: (c, 0, g))
    st_spec = pl.BlockSpec((n, n, LANES), lambda g, c: (0, 0, g))
    return pl.pallas_call(
        functools.partial(_wkv_kernel, ts=ts, n=n),
        grid=(l // LANES, s // ts),
        in_specs=[seq_spec] * 6 + [st_spec],
        out_specs=[seq_spec, st_spec],
        out_shape=[jax.ShapeDtypeStruct((s, n, l), F32), jax.ShapeDtypeStruct((n, n, l), F32)],
        scratch_shapes=[pltpu.VMEM((n, n, LANES), F32)],
        compiler_params=_params("parallel", "arbitrary"),
    )(r, w, k, v, a, b, s0)


def _rope_tables(pos, half):
    inv = ROPE_THETA ** (-jnp.arange(half, dtype=F32) / half)
    ang = pos.astype(F32)[:, None] * inv[None, :]
    return jnp.cos(ang), jnp.sin(ang)


def _rope(x, cos, sin, n_rot):
    half = n_rot // 2
    shape = (x.shape[0],) + (1,) * (x.ndim - 2) + (half,)
    cos, sin = cos.reshape(shape), sin.reshape(shape)
    x1, x2 = x[..., :half], x[..., half:n_rot]
    return jnp.concatenate([x1 * cos - x2 * sin, x2 * cos + x1 * sin, x[..., n_rot:]], axis=-1)


def _rms(x, g, eps):
    return x * lax.rsqrt(jnp.mean(jnp.square(x), axis=-1, keepdims=True) + eps) * g


class _Tokens:
    def __init__(self, batch, seq, dec_batch, past_len):
        self.batch, self.seq, self.dec_batch, self.past_len = batch, seq, dec_batch, past_len
        self.n_prompt = batch * seq
        self.n_tokens = self.n_prompt + dec_batch
        self.pos = jnp.concatenate([jnp.tile(jnp.arange(seq, dtype=jnp.int32), batch),
                                    jnp.full((dec_batch,), past_len, jnp.int32)])


def _mla_mixer(tk, h, p, cache_ckv, cache_kpe, page_table):
    n_p, nb, seq, ns = tk.n_prompt, tk.batch, tk.seq, tk.dec_batch
    c_dim = p['w_dkv'].shape[1] - QK_ROPE
    nh = p['w_ukv'].shape[1]
    qd = QK_NOPE + QK_ROPE
    scale = qd ** -0.5
    cos, sin = _rope_tables(tk.pos, QK_ROPE // 2)

    cq = _rms(matmul(h, p['w_dq']), p['q_norm'], MLA_EPS).astype(BF16)
    w_uq = p['w_uq'].reshape(-1, nh, qd)
    w_uq = jnp.concatenate([w_uq[..., :QK_NOPE].reshape(-1, nh * QK_NOPE),
                            w_uq[..., QK_NOPE:].reshape(-1, nh * QK_ROPE)], axis=1)
    q = matmul(cq, w_uq)
    q_nope = q[:, :nh * QK_NOPE].astype(BF16)
    q_pe = _rope(q[:, nh * QK_NOPE:].reshape(-1, nh, QK_ROPE), cos, sin, QK_ROPE).astype(BF16)
    q_pe = q_pe.reshape(-1, nh * QK_ROPE)

    kv = matmul(h, p['w_dkv'])
    ckv = _rms(kv[:, :c_dim], p['kv_norm'], MLA_EPS)
    kpe = _rope(kv[:, c_dim:], cos, sin, QK_ROPE)
    w_ukv2 = p['w_ukv'].reshape(c_dim, nh * (QK_NOPE + V_HEAD))

    kvb = matmul(ckv[:n_p].astype(BF16), w_ukv2, out_dtype=BF16)
    kpe_b = kpe[:n_p].astype(BF16)
    zeros = jnp.zeros_like(kpe_b)
    kpe2 = jnp.concatenate([kpe_b, zeros, zeros, kpe_b], axis=-1)
    o_p = mla_flash(q_nope[:n_p].reshape(nb, seq, -1), q_pe[:n_p].reshape(nb, seq, -1),
                    kvb.reshape(nb, seq, -1), kpe2.reshape(nb, seq, -1), scale)

    q_lat = absorb_q(q_nope[n_p:], w_ukv2).reshape(ns, nh, c_dim)
    o_lat = mla_decode(page_table, q_lat, q_pe[n_p:].reshape(ns, nh, QK_ROPE),
                       ckv[n_p:].reshape(ns, 1, c_dim), kpe[n_p:].reshape(ns, 1, QK_ROPE),
                       cache_ckv, cache_kpe, scale)
    o_s = expand_o(o_lat.reshape(ns, nh * c_dim), w_ukv2)

    o = jnp.concatenate([o_p.reshape(n_p, -1), o_s], axis=0)
    out = matmul(o, p['w_o'])
    return out, (ckv[:n_p].reshape(nb, seq, c_dim), kpe[:n_p].reshape(nb, seq, QK_ROPE),
                 ckv[n_p:].reshape(ns, 1, c_dim), kpe[n_p:].reshape(ns, 1, QK_ROPE))


def _diff_mixer(tk, h, p, cache_k, cache_v, page_table, layer_idx):
    n_p, nb, seq, ns = tk.n_prompt, tk.batch, tk.seq, tk.dec_batch
    w = 2 * DIFF_HEAD
    nh = p['w_q'].shape[1] // w
    scale = DIFF_HEAD ** -0.5
    cos, sin = _rope_tables(tk.pos, DIFF_ROT // 2)
    lam_init = 0.8 - 0.6 * math.exp(-0.3 * layer_idx)
    lam = (jnp.exp(jnp.sum(p['lam_q1'] * p['lam_k1'])) - jnp.exp(jnp.sum(p['lam_q2'] * p['lam_k2'])) + lam_init)
    lam_row = jnp.full((1, w), lam, F32)
    gain_row = (p['subln'] * (1.0 - lam_init)).reshape(1, w)

    q = _rope(matmul(h, p['w_q']).reshape(-1, nh, 2, DIFF_HEAD), cos, sin, DIFF_ROT).astype(BF16)
    k = _rope(matmul(h, p['w_k']).reshape(-1, 2, DIFF_HEAD), cos, sin, DIFF_ROT)
    v = matmul(h, p['w_v'])
    k2 = k.reshape(-1, w)

    o_p = diff_flash(q[:n_p].reshape(nb, seq, nh * w), k2[:n_p].astype(BF16).reshape(nb, seq, w),
                     v[:n_p].astype(BF16).reshape(nb, seq, w), lam_row, gain_row, scale)
    q_s = jnp.swapaxes(q[n_p:], 1, 2)
    o_s = diff_decode(page_table, q_s, k2[n_p:].reshape(ns, 1, w), v[n_p:].reshape(ns, 1, w),
                      lam_row, gain_row, cache_k.reshape(-1, PAGE, w), cache_v, scale)
    o = jnp.concatenate([o_p.reshape(n_p, -1), o_s.reshape(ns, -1)], axis=0)
    out = matmul(o, p['w_o'])
    return out, (k[:n_p].reshape(nb, seq, 2, DIFF_HEAD), v[:n_p].reshape(nb, seq, w),
                 k[n_p:].reshape(ns, 1, 2, DIFF_HEAD), v[n_p:].reshape(ns, 1, w))


def _rwkv_mixer(tk, h, p, shift0, wkv0):
    n_p, nb, seq, ns = tk.n_prompt, tk.batch, tk.seq, tk.dec_batch
    d = h.shape[1]
    n = RWKV_HEAD
    nh = d // n
    hp = h[:n_p].reshape(nb, seq, d)
    prev = jnp.concatenate([jnp.zeros((nb, 1, d), F32), hp[:, :-1]], axis=1).reshape(n_p, d)
    prev = jnp.concatenate([prev, shift0], axis=0)
    xx = prev - h
    xr, xw, xk, xv, xa, xg = ((h + xx * p['mix'][j]).astype(BF16) for j in range(6))
    r = matmul(xr, p['w_r'])
    k = matmul(xk, p['w_k'])
    v = matmul(xv, p['w_v'])
    lw = matmul(matmul(xw, p['w1'], act="tanh", out_dtype=BF16), p['w2'], bias=p['w0'])
    log_w = -jax.nn.softplus(-lw) - 0.5
    decay = jnp.exp(-jnp.exp(log_w))
    a = matmul(matmul(xa, p['a1'], out_dtype=BF16), p['a2'], bias=p['a0'], act="sigmoid")
    g = matmul(matmul(xg, p['g1'], act="sigmoid", out_dtype=BF16), p['g2'])
    kk = (k * p['k_k']).reshape(-1, nh, n)
    kk = kk / jnp.maximum(jnp.sqrt(jnp.sum(jnp.square(kk), axis=-1, keepdims=True)), 1e-12)
    k = k * (1.0 + (a - 1.0) * p['k_a'])
    aa = -kk
    bb = kk * a.reshape(-1, nh, n)

    def lanes_p(t):
        return t[:n_p].reshape(nb, seq, nh, n).transpose(1, 3, 0, 2).reshape(seq, n, nb * nh)

    def lanes_s(t):
        return t[n_p:].reshape(ns, nh, n).transpose(2, 0, 1).reshape(1, n, ns * nh)

    seqs = (r, decay, k, v, aa.reshape(-1, d), bb.reshape(-1, d))
    y_p, st_p = wkv_scan(*(lanes_p(t) for t in seqs), jnp.zeros((n, n, nb * nh), F32))
    s0_s = wkv0.transpose(3, 2, 0, 1).reshape(n, n, ns * nh)
    y_s, st_s = wkv_scan(*(lanes_s(t) for t in seqs), s0_s)
    y = jnp.concatenate([y_p.reshape(seq, n, nb, nh).transpose(2, 0, 3, 1).reshape(n_p, nh, n),
                         y_s.reshape(n, ns, nh).transpose(1, 2, 0)], axis=0)
    wkv_p = st_p.reshape(n, n, nb, nh).transpose(2, 3, 1, 0)
    wkv_s = st_s.reshape(n, n, ns, nh).transpose(2, 3, 1, 0)

    mu = jnp.mean(y, axis=-1, keepdims=True)
    var = jnp.mean(jnp.square(y - mu), axis=-1, keepdims=True)
    yn = ((y - mu) * lax.rsqrt(var + RWKV_GN_EPS)).reshape(-1, d) * p['lnx_g'] + p['lnx_b']
    bonus = jnp.sum((r * k).reshape(-1, nh, n) * p['r_k'], axis=-1, keepdims=True) * v.reshape(-1, nh, n)
    yo = ((yn + bonus.reshape(-1, d)) * g).astype(BF16)
    out = matmul(yo, p['w_o'])
    return out, (hp[:, -1], wkv_p, h[n_p:], wkv_s)


def _dense_ffn(h, w1, w3, w2):
    return matmul(swiglu_up(h, w1, w3), w2)


def _moe_ffn(h, router, w1, w3, w2, tm=512):
    t, d = h.shape
    n_exp = router.shape[1]
    logits = matmul(h, router)
    top_v, top_i = lax.top_k(logits, TOP_K)
    gates = jax.nn.softmax(top_v, axis=-1)

    n_assign = t * TOP_K
    n_tiles = pl.cdiv(n_assign, tm) + n_exp
    n_slots = n_tiles * tm
    flat_e = top_i.reshape(-1).astype(jnp.int32)
    order = jnp.argsort(flat_e, stable=True).astype(jnp.int32)
    sorted_e = flat_e[order]
    counts = jnp.zeros((n_exp,), jnp.int32).at[flat_e].add(1)
    padded = ((counts + tm - 1) // tm) * tm
    pad_end = jnp.cumsum(padded)
    pad_start = pad_end - padded
    start = jnp.cumsum(counts) - counts
    dest = pad_start[sorted_e] + (jnp.arange(n_assign, dtype=jnp.int32) - start[sorted_e])
    slot_tok = jnp.zeros((n_slots,), jnp.int32).at[dest].set(order // TOP_K)
    slot_gate = jnp.zeros((n_slots,), F32).at[dest].set(gates.reshape(-1)[order])
    slot_of = jnp.zeros((n_assign,), jnp.int32).at[order].set(dest)
    tile_start = jnp.arange(n_tiles, dtype=jnp.int32) * tm
    tile_expert = jnp.minimum(jnp.searchsorted(pad_end, tile_start, side='right'), n_exp - 1).astype(jnp.int32)
    n_used = (pad_end[-1:] // tm).astype(jnp.int32)

    h32 = lax.bitcast_convert_type(h.reshape(t, d // 2, 2), jnp.uint32)
    xg = lax.bitcast_convert_type(gather_rows(slot_tok, h32), BF16).reshape(n_slots, d)
    mid = grouped_swiglu_up(tile_expert, n_used, xg, w1, w3, tm)
    y = grouped_down(tile_expert, n_used, mid, w2, slot_gate.reshape(n_slots, 1), tm)
    return combine_pairs(slot_of, y)


def _forward(x_prompt, x_sample, page_table, c_prompt, c_sample, layers):
    nb, seq, d = x_prompt.shape
    ns = x_sample.shape[0]
    n_pages = page_table.shape[1]
    tk = _Tokens(nb, seq, ns, n_pages * PAGE)
    depth = len(layers)
    alpha = (2 * depth) ** 0.25
    x = jnp.concatenate([x_prompt.reshape(nb * seq, d), x_sample.reshape(ns, d)], axis=0)

    pad = (-(ns + nb)) % 16
    c_all = jax.nn.silu(jnp.concatenate([c_sample, c_prompt, jnp.zeros((pad, d), F32)], axis=0)).astype(BF16)
    mods = []
    for L in layers:
        mod = matmul(c_all, L['ada'][0], bias=L['ada'][1])
        chunks = []
        for j in range(6):
            col = mod[:, j * d:(j + 1) * d]
            chunks.append((col[ns:ns + nb].reshape(nb, 1, d), col[:ns]))
        mods.append(chunks)

    nm = functools.partial(norm_mod, n_prompt=tk.n_prompt, seq=seq)
    outs = []
    h_dtypes = lambda i: (BF16, F32) if layers[i]['kind'] == 'rwkv7' else (BF16,)
    hs = nm(x, scale=mods[0][1], shift=mods[0][0], h_dtypes=h_dtypes(0))
    for i, L in enumerate(layers):
        m = mods[i]
        kind = L['kind']
        if kind == 'mla':
            o, new = _mla_mixer(tk, hs[0], L['mix'], L['state'][0], L['state'][1], page_table)
        elif kind == 'diff':
            o, new = _diff_mixer(tk, hs[0], L['mix'], L['state'][0], L['state'][1], page_table, i)
        else:
            o, new = _rwkv_mixer(tk, hs[1], L['mix'], L['state'][0], L['state'][1])
        outs.append(new)
        x, h2 = nm(x, sub=o, gate=m[2], ln=L['ln'][:2], scale=m[4], shift=m[3], alpha=alpha)
        if len(L['ffn']) == 3:
            f = _dense_ffn(h2, *L['ffn'])
        else:
            f = _moe_ffn(h2, *L['ffn'])
        if i + 1 < depth:
            res = nm(x, sub=f, gate=m[5], ln=L['ln'][2:], scale=mods[i + 1][1], shift=mods[i + 1][0],
                     h_dtypes=h_dtypes(i + 1), alpha=alpha)
            x, hs = res[0], res[1:]
        else:
            x, = nm(x, sub=f, gate=m[5], ln=L['ln'][2:], alpha=alpha)
    y_prompt = x[:tk.n_prompt].reshape(nb, seq, d)
    y_sample = x[tk.n_prompt:].reshape(ns, 1, d)
    flat = [y_prompt, y_sample]
    for new in outs:
        flat.extend(new)
    return tuple(flat)


def kernel(x_prompt, x_sample, cache_l0_ckv, cache_l0_kpe, cache_l1_k, cache_l1_v, state_l2_shift, state_l2_wkv, cache_l3_ckv, cache_l3_kpe, page_table, c_prompt, c_sample, l0_ada_w, l0_ada_b, l0_ln1_g, l0_ln1_b, l0_ln2_g, l0_ln2_b, l0_w_dq, l0_q_norm, l0_w_uq, l0_w_dkv, l0_kv_norm, l0_w_ukv, l0_w_o, l0_ffn_w1, l0_ffn_w3, l0_ffn_w2, l1_ada_w, l1_ada_b, l1_ln1_g, l1_ln1_b, l1_ln2_g, l1_ln2_b, l1_w_q, l1_w_k, l1_w_v, l1_lam_q1, l1_lam_k1, l1_lam_q2, l1_lam_k2, l1_subln, l1_w_o, l1_router, l1_exp_w1, l1_exp_w3, l1_exp_w2, l2_ada_w, l2_ada_b, l2_ln1_g, l2_ln1_b, l2_ln2_g, l2_ln2_b, l2_mix, l2_w_r, l2_w_k, l2_w_v, l2_w_o, l2_w0, l2_w1, l2_w2, l2_a0, l2_a1, l2_a2, l2_g1, l2_g2, l2_k_k, l2_k_a, l2_r_k, l2_lnx_g, l2_lnx_b, l2_ffn_w1, l2_ffn_w3, l2_ffn_w2, l3_ada_w, l3_ada_b, l3_ln1_g, l3_ln1_b, l3_ln2_g, l3_ln2_b, l3_w_dq, l3_q_norm, l3_w_uq, l3_w_dkv, l3_kv_norm, l3_w_ukv, l3_w_o, l3_router, l3_exp_w1, l3_exp_w3, l3_exp_w2):
    layers = (
        dict(kind='mla', ada=(l0_ada_w, l0_ada_b), ln=(l0_ln1_g, l0_ln1_b, l0_ln2_g, l0_ln2_b),
             mix=dict(w_dq=l0_w_dq, q_norm=l0_q_norm, w_uq=l0_w_uq, w_dkv=l0_w_dkv, kv_norm=l0_kv_norm,
                      w_ukv=l0_w_ukv, w_o=l0_w_o),
             state=(cache_l0_ckv, cache_l0_kpe), ffn=(l0_ffn_w1, l0_ffn_w3, l0_ffn_w2)),
        dict(kind='diff', ada=(l1_ada_w, l1_ada_b), ln=(l1_ln1_g, l1_ln1_b, l1_ln2_g, l1_ln2_b),
             mix=dict(w_q=l1_w_q, w_k=l1_w_k, w_v=l1_w_v, lam_q1=l1_lam_q1, lam_k1=l1_lam_k1,
                      lam_q2=l1_lam_q2, lam_k2=l1_lam_k2, subln=l1_subln, w_o=l1_w_o),
             state=(cache_l1_k, cache_l1_v), ffn=(l1_router, l1_exp_w1, l1_exp_w3, l1_exp_w2)),
        dict(kind='rwkv7', ada=(l2_ada_w, l2_ada_b), ln=(l2_ln1_g, l2_ln1_b, l2_ln2_g, l2_ln2_b),
             mix=dict(mix=l2_mix, w_r=l2_w_r, w_k=l2_w_k, w_v=l2_w_v, w_o=l2_w_o, w0=l2_w0, w1=l2_w1, w2=l2_w2,
                      a0=l2_a0, a1=l2_a1, a2=l2_a2, g1=l2_g1, g2=l2_g2, k_k=l2_k_k, k_a=l2_k_a, r_k=l2_r_k,
                      lnx_g=l2_lnx_g, lnx_b=l2_lnx_b),
             state=(state_l2_shift, state_l2_wkv), ffn=(l2_ffn_w1, l2_ffn_w3, l2_ffn_w2)),
        dict(kind='mla', ada=(l3_ada_w, l3_ada_b), ln=(l3_ln1_g, l3_ln1_b, l3_ln2_g, l3_ln2_b),
             mix=dict(w_dq=l3_w_dq, q_norm=l3_q_norm, w_uq=l3_w_uq, w_dkv=l3_w_dkv, kv_norm=l3_kv_norm,
                      w_ukv=l3_w_ukv, w_o=l3_w_o),
             state=(cache_l3_ckv, cache_l3_kpe), ffn=(l3_router, l3_exp_w1, l3_exp_w3, l3_exp_w2)),
    )
    return _forward(x_prompt, x_sample, page_table, c_prompt, c_sample, layers)
```

```python
import functools
import math

import jax
import jax.numpy as jnp
from jax import lax
from jax.experimental import pallas as pl
from jax.experimental.pallas import tpu as pltpu

F32 = jnp.float32
BF16 = jnp.bfloat16

LANES = 128
VMEM_LIMIT_BYTES = 56 * 1024 * 1024

ROPE_THETA = 500000.0
LN_EPS = 1e-5
MLA_EPS = 1e-6
DIFF_EPS = 1e-5
RWKV_GN_EPS = 64e-5
QK_NOPE = 128
QK_ROPE = 64
V_HEAD = 128
DIFF_HEAD = 128
DIFF_ROT = DIFF_HEAD // 4
RWKV_HEAD = 64
TOP_K = 2
PAGE = 128


def _params(*sem):
    return pltpu.CompilerParams(dimension_semantics=sem, vmem_limit_bytes=VMEM_LIMIT_BYTES)


def _row_tile(m, cap):
    if m <= cap:
        return m
    for t in range(cap - cap % 16, 15, -16):
        if m % t == 0:
            return t
    return cap


def _act(v, act):
    if act == "tanh":
        return jnp.tanh(v)
    if act == "sigmoid":
        return jax.nn.sigmoid(v)
    return v


def _matmul_kernel(*refs, nk, has_bias, act):
    if has_bias:
        x_ref, w_ref, b_ref, o_ref = refs
    else:
        x_ref, w_ref, o_ref = refs
        b_ref = None
    acc = jnp.dot(x_ref[...].astype(BF16), w_ref[...].astype(BF16), preferred_element_type=F32)

    def finish(v):
        if has_bias:
            v = v + b_ref[...]
        return _act(v, act).astype(o_ref.dtype)

    if nk == 1:
        o_ref[...] = finish(acc)
    else:
        k = pl.program_id(2)

        @pl.when(k == 0)
        def _():
            o_ref[...] = acc

        @pl.when(jnp.logical_and(k > 0, k < nk - 1))
        def _():
            o_ref[...] += acc

        @pl.when(k == nk - 1)
        def _():
            o_ref[...] = finish(o_ref[...] + acc)


def _k_split(k):
    if k <= 4096:
        return 1
    for nk in range(2, 65):
        if k % nk == 0 and (k // nk) % LANES == 0 and k // nk <= 5504:
            return nk
    raise ValueError(f"no K split for {k}")


def matmul(x, w, bias=None, act=None, out_dtype=F32, tm_cap=1024, split_k=True):
    m, k = x.shape
    n = w.shape[1]
    nk = _k_split(k) if split_k else 1
    tk = k // nk
    tm = _row_tile(m, tm_cap)
    if nk > 1:
        tn = 256
        assert out_dtype == F32
    else:
        tn = n if n <= 640 else 512
    grid = (pl.cdiv(m, tm), pl.cdiv(n, tn), nk)
    in_specs = [pl.BlockSpec((tm, tk), lambda i, j, kk: (i, kk)),
                pl.BlockSpec((tk, tn), lambda i, j, kk: (kk, j))]
    args = [x, w]
    if bias is not None:
        in_specs.append(pl.BlockSpec((1, tn), lambda i, j, kk: (0, j)))
        args.append(bias.reshape(1, n).astype(F32))
    return pl.pallas_call(
        functools.partial(_matmul_kernel, nk=nk, has_bias=bias is not None, act=act),
        grid=grid, in_specs=in_specs,
        out_specs=pl.BlockSpec((tm, tn), lambda i, j, kk: (i, j)),
        out_shape=jax.ShapeDtypeStruct((m, n), out_dtype),
        compiler_params=_params("parallel", "parallel", "arbitrary"),
    )(*args)


def _swiglu_kernel(x_ref, w1_ref, w3_ref, o_ref):
    x = x_ref[...].astype(BF16)
    a = jnp.dot(x, w1_ref[...].astype(BF16), preferred_element_type=F32)
    b = jnp.dot(x, w3_ref[...].astype(BF16), preferred_element_type=F32)
    o_ref[...] = (a * jax.nn.sigmoid(a) * b).astype(o_ref.dtype)


def swiglu_up(x, w1, w3, tm_cap=1024, tn=256):
    m, k = x.shape
    n = w1.shape[1]
    tm = _row_tile(m, tm_cap)
    return pl.pallas_call(
        _swiglu_kernel,
        grid=(pl.cdiv(m, tm), pl.cdiv(n, tn)),
        in_specs=[pl.BlockSpec((tm, k), lambda i, j: (i, 0)),
                  pl.BlockSpec((k, tn), lambda i, j: (0, j)),
                  pl.BlockSpec((k, tn), lambda i, j: (0, j))],
        out_specs=pl.BlockSpec((tm, tn), lambda i, j: (i, j)),
        out_shape=jax.ShapeDtypeStruct((m, n), BF16),
        compiler_params=_params("parallel", "parallel"),
    )(x, w1, w3)


def _grouped_swiglu_kernel(te_ref, nu_ref, x_ref, w1_ref, w3_ref, o_ref):
    i = pl.program_id(1)

    @pl.when(i < nu_ref[0])
    def _():
        x = x_ref[...]
        a = jnp.dot(x, w1_ref[...].astype(BF16), preferred_element_type=F32)
        b = jnp.dot(x, w3_ref[...].astype(BF16), preferred_element_type=F32)
        o_ref[...] = (a * jax.nn.sigmoid(a) * b).astype(o_ref.dtype)

    @pl.when(i >= nu_ref[0])
    def _():
        o_ref[...] = jnp.zeros_like(o_ref)


def grouped_swiglu_up(tile_expert, n_used, xg, w1, w3, tm, tn=256):
    p, k = xg.shape
    n = w1.shape[2]
    grid_spec = pltpu.PrefetchScalarGridSpec(
        num_scalar_prefetch=2, grid=(n // tn, p // tm),
        in_specs=[pl.BlockSpec((tm, k), lambda j, i, te, nu: (i, 0)),
                  pl.BlockSpec((None, k, tn), lambda j, i, te, nu: (te[i], 0, j)),
                  pl.BlockSpec((None, k, tn), lambda j, i, te, nu: (te[i], 0, j))],
        out_specs=pl.BlockSpec((tm, tn), lambda j, i, te, nu: (i, j)))
    return pl.pallas_call(
        _grouped_swiglu_kernel, grid_spec=grid_spec,
        out_shape=jax.ShapeDtypeStruct((p, n), BF16),
        compiler_params=_params("parallel", "arbitrary"),
    )(tile_expert, n_used, xg, w1, w3)


def _grouped_down_kernel(te_ref, nu_ref, h_ref, w_ref, g_ref, o_ref):
    i = pl.program_id(1)

    @pl.when(i < nu_ref[0])
    def _():
        y = jnp.dot(h_ref[...], w_ref[...].astype(BF16), preferred_element_type=F32)
        o_ref[...] = y * g_ref[...]

    @pl.when(i >= nu_ref[0])
    def _():
        o_ref[...] = jnp.zeros_like(o_ref)


def grouped_down(tile_expert, n_used, h, w2, gate_col, tm, tn=512):
    p, k = h.shape
    n = w2.shape[2]
    grid_spec = pltpu.PrefetchScalarGridSpec(
        num_scalar_prefetch=2, grid=(n // tn, p // tm),
        in_specs=[pl.BlockSpec((tm, k), lambda j, i, te, nu: (i, 0)),
                  pl.BlockSpec((None, k, tn), lambda j, i, te, nu: (te[i], 0, j)),
                  pl.BlockSpec((tm, 1), lambda j, i, te, nu: (i, 0))],
        out_specs=pl.BlockSpec((tm, tn), lambda j, i, te, nu: (i, j)))
    return pl.pallas_call(
        _grouped_down_kernel, grid_spec=grid_spec,
        out_shape=jax.ShapeDtypeStruct((p, n), F32),
        compiler_params=_params("parallel", "arbitrary"),
    )(tile_expert, n_used, h, w2, gate_col)


def _gather_rows_kernel(idx_ref, x_hbm, o_ref, buf, sem, *, rows):
    base = pl.program_id(0) * rows

    def row_copy(r, src):
        return pltpu.make_async_copy(x_hbm.at[pl.ds(src, 1)], buf.at[pl.ds(r, 1)], sem)

    def issue(r, carry):
        row_copy(r, idx_ref[base + r]).start()
        return carry

    def drain(r, carry):
        row_copy(r, 0).wait()
        return carry

    lax.fori_loop(0, rows, issue, 0)
    lax.fori_loop(0, rows, drain, 0)
    o_ref[...] = buf[...].astype(o_ref.dtype)


def gather_rows(idx, x, out_dtype, rows=128):
    p = idx.shape[0]
    c = x.shape[1]
    rows = _row_tile(p, rows)
    grid_spec = pltpu.PrefetchScalarGridSpec(
        num_scalar_prefetch=1, grid=(p // rows,),
        in_specs=[pl.BlockSpec(memory_space=pl.ANY)],
        out_specs=pl.BlockSpec((rows, c), lambda i, idx_ref: (i, 0)),
        scratch_shapes=[pltpu.VMEM((rows, c), x.dtype), pltpu.SemaphoreType.DMA(())])
    return pl.pallas_call(
        functools.partial(_gather_rows_kernel, rows=rows), grid_spec=grid_spec,
        out_shape=jax.ShapeDtypeStruct((p, c), out_dtype),
        compiler_params=_params("arbitrary"),
    )(idx, x)


def _combine_kernel(idx_ref, y_hbm, o_ref, buf0, buf1, sem, *, rows):
    base = pl.program_id(0) * rows

    def row_copy(r, src, buf):
        return pltpu.make_async_copy(y_hbm.at[pl.ds(src, 1)], buf.at[pl.ds(r, 1)], sem)

    def issue(r, carry):
        row_copy(r, idx_ref[2 * (base + r)], buf0).start()
        row_copy(r, idx_ref[2 * (base + r) + 1], buf1).start()
        return carry

    def drain(r, carry):
        row_copy(r, 0, buf0).wait()
        row_copy(r, 0, buf1).wait()
        return carry

    lax.fori_loop(0, rows, issue, 0)
    lax.fori_loop(0, rows, drain, 0)
    o_ref[...] = buf0[...] + buf1[...]


def combine_pairs(idx2, y, rows=128):
    t = idx2.shape[0] // 2
    c = y.shape[1]
    rows = _row_tile(t, rows)
    grid_spec = pltpu.PrefetchScalarGridSpec(
        num_scalar_prefetch=1, grid=(t // rows,),
        in_specs=[pl.BlockSpec(memory_space=pl.ANY)],
        out_specs=pl.BlockSpec((rows, c), lambda i, idx_ref: (i, 0)),
        scratch_shapes=[pltpu.VMEM((rows, c), F32), pltpu.VMEM((rows, c), F32),
                        pltpu.SemaphoreType.DMA(())])
    return pl.pallas_call(
        functools.partial(_combine_kernel, rows=rows), grid_spec=grid_spec,
        out_shape=jax.ShapeDtypeStruct((t, c), F32),
        compiler_params=_params("arbitrary"),
    )(idx2, y)


def _norm_mod_kernel(*refs, alpha, do_norm, do_mod, n_prompt_tiles, rows_per_batch_tiles):
    it = iter(refs)
    x_ref = next(it)
    is_sample = pl.program_id(0) >= n_prompt_tiles

    def pick(p_ref, s_ref):
        return jnp.where(is_sample, s_ref[...], p_ref[0])

    if do_norm:
        sub_ref, gp_ref, gs_ref, g_ref, b_ref = (next(it) for _ in range(5))
    if do_mod:
        scp_ref, scs_ref, shp_ref, shs_ref = (next(it) for _ in range(4))
    outs = list(it)
    x = x_ref[...]
    if do_norm:
        z = alpha * x + (1.0 + pick(gp_ref, gs_ref)) * sub_ref[...]
        mu = jnp.mean(z, axis=-1, keepdims=True)
        zc = z - mu
        var = jnp.mean(zc * zc, axis=-1, keepdims=True)
        x = zc * lax.rsqrt(var + LN_EPS) * g_ref[...] + b_ref[...]
        outs.pop(0)[...] = x
    if do_mod:
        h = x * (1.0 + pick(scp_ref, scs_ref)) + pick(shp_ref, shs_ref)
        for o_ref in outs:
            o_ref[...] = h.astype(o_ref.dtype)


def norm_mod(x, *, n_prompt, seq, sub=None, gate=None, ln=None, scale=None, shift=None,
             h_dtypes=(BF16,), alpha=1.0, rows=128):
    nt, d = x.shape
    ns = nt - n_prompt
    rows = min(rows, ns)
    assert ns % rows == 0 and seq % rows == 0
    n_prompt_tiles = n_prompt // rows
    tiles_per_batch = seq // rows
    nb = n_prompt // seq
    row_spec = pl.BlockSpec((rows, d), lambda i: (i, 0))
    pb_spec = pl.BlockSpec((1, 1, d), lambda i: (jnp.minimum(i // tiles_per_batch, nb - 1), 0, 0))
    ps_spec = pl.BlockSpec((rows, d), lambda i: (jnp.maximum(i - n_prompt_tiles, 0), 0))
    vec_spec = pl.BlockSpec((1, d), lambda i: (0, 0))
    args, in_specs = [x], [row_spec]
    do_norm, do_mod = sub is not None, scale is not None
    out_shape, out_specs = [], []
    if do_norm:
        args += [sub, gate[0], gate[1], ln[0].reshape(1, d), ln[1].reshape(1, d)]
        in_specs += [row_spec, pb_spec, ps_spec, vec_spec, vec_spec]
        out_shape.append(jax.ShapeDtypeStruct((nt, d), F32))
        out_specs.append(row_spec)
    if do_mod:
        args += [scale[0], scale[1], shift[0], shift[1]]
        in_specs += [pb_spec, ps_spec, pb_spec, ps_spec]
        for dt in h_dtypes:
            out_shape.append(jax.ShapeDtypeStruct((nt, d), dt))
            out_specs.append(row_spec)
    return pl.pallas_call(
        functools.partial(_norm_mod_kernel, alpha=alpha, do_norm=do_norm, do_mod=do_mod,
                          n_prompt_tiles=n_prompt_tiles, rows_per_batch_tiles=tiles_per_batch),
        grid=(nt // rows,), in_specs=in_specs, out_specs=out_specs, out_shape=out_shape,
        compiler_params=_params("parallel"),
    )(*args)


LOG2E = math.log2(math.e)


def _causal_mask(s, qi, ki, tq, tk):
    row = qi * tq + lax.broadcasted_iota(jnp.int32, s.shape, 0)
    col = ki * tk + lax.broadcasted_iota(jnp.int32, s.shape, 1)
    return jnp.where(row >= col, s, -jnp.inf)


def _dot_nt(a, b):
    return lax.dot_general(a, b, (((1,), (1,)), ((), ())), preferred_element_type=F32)


def _lane_tiles(s):
    return [s[:, j * LANES:(j + 1) * LANES] for j in range(s.shape[1] // LANES)]


def _online_softmax_tiles(s, m_prev, c):
    tiles = _lane_tiles(s)
    mx = functools.reduce(jnp.maximum, tiles)
    m_new = jnp.maximum(m_prev, jnp.max(mx, axis=-1, keepdims=True))
    alpha = jnp.exp2((m_prev - m_new) * c)
    return m_new, alpha, [jnp.exp2((t - m_new) * c) for t in tiles]


def _causal_pairs(n):
    pairs = [(qi, ki) for qi in range(n) for ki in range(qi + 1)]
    return (jnp.asarray([p[0] for p in pairs], jnp.int32), jnp.asarray([p[1] for p in pairs], jnp.int32))


def _mla_flash_kernel(qt_ref, kt_ref, qn_ref, qp_ref, kv_ref, kpe_ref, o_ref, m_ref, acc_ref, *, c, tq, tk):
    st = pl.program_id(2)
    qi, ki = qt_ref[st], kt_ref[st]

    @pl.when(ki == 0)
    def _():
        m_ref[...] = jnp.full_like(m_ref, -jnp.inf)
        acc_ref[...] = jnp.zeros_like(acc_ref)

    def update(masked):
        qp = qp_ref[0]
        ones = jnp.ones((tk, V_HEAD), BF16)
        for h in range(2):
            q = jnp.concatenate([qn_ref[0, :, h * QK_NOPE:(h + 1) * QK_NOPE], qp], axis=-1)
            k = jnp.concatenate([kv_ref[0, :, h * 256:h * 256 + QK_NOPE],
                                 kpe_ref[0, :, h * LANES:(h + 1) * LANES]], axis=-1)
            v1 = jnp.concatenate([kv_ref[0, :, h * 256 + QK_NOPE:(h + 1) * 256], ones], axis=-1)
            s = _dot_nt(q, k)
            if masked:
                s = _causal_mask(s, qi, ki, tq, tk)
            m_new, alpha, p = _online_softmax_tiles(s, m_ref[h], c)
            pv = jnp.dot(jnp.concatenate(p, axis=-1).astype(BF16), v1, preferred_element_type=F32)
            acc_ref[h] = jnp.concatenate([alpha, alpha], axis=-1) * acc_ref[h] + pv
            m_ref[h] = m_new

    @pl.when(ki < qi)
    def _():
        update(False)

    @pl.when(ki == qi)
    def _():
        update(True)
        for h in range(2):
            acc = acc_ref[h]
            o_ref[0, :, h * V_HEAD:(h + 1) * V_HEAD] = (acc[:, :V_HEAD] / acc[:, V_HEAD:]).astype(o_ref.dtype)


def mla_flash(qn, qp, kvb, kpe2, scale, tq=512):
    b, s, hd = qn.shape
    hp = hd // (2 * QK_NOPE)
    tq = min(tq, s)
    tk = tq
    qt, kt = _causal_pairs(s // tq)
    q_idx = lambda bi, h, st, qt_ref, kt_ref: (bi, qt_ref[st], h)
    grid_spec = pltpu.PrefetchScalarGridSpec(
        num_scalar_prefetch=2, grid=(b, hp, qt.shape[0]),
        in_specs=[pl.BlockSpec((1, tq, 2 * QK_NOPE), q_idx),
                  pl.BlockSpec((1, tq, 2 * QK_ROPE), q_idx),
                  pl.BlockSpec((1, tk, 512), lambda bi, h, st, qt_ref, kt_ref: (bi, kt_ref[st], h)),
                  pl.BlockSpec((1, tk, 256), lambda bi, h, st, qt_ref, kt_ref: (bi, kt_ref[st], 0))],
        out_specs=pl.BlockSpec((1, tq, 2 * V_HEAD), q_idx),
        scratch_shapes=[pltpu.VMEM((2, tq, LANES), F32), pltpu.VMEM((2, tq, 2 * V_HEAD), F32)])
    return pl.pallas_call(
        functools.partial(_mla_flash_kernel, c=scale * LOG2E, tq=tq, tk=tk), grid_spec=grid_spec,
        out_shape=jax.ShapeDtypeStruct((b, s, hp * 2 * V_HEAD), BF16),
        compiler_params=_params("parallel", "parallel", "arbitrary"),
    )(qt, kt, qn, qp, kvb, kpe2)


def _diff_finish(acc_ref, l_ref, lam_ref, g_ref):
    o = acc_ref[0] / l_ref[0] - lam_ref[...] * (acc_ref[1] / l_ref[1])
    return o * lax.rsqrt(jnp.mean(o * o, axis=-1, keepdims=True) + DIFF_EPS) * g_ref[...]


def _diff_flash_kernel(qt_ref, kt_ref, q_ref, k_ref, v_ref, lam_ref, g_ref, o_ref, m_ref, l_ref, acc_ref,
                       *, c, tq, tk):
    st = pl.program_id(2)
    qi, ki = qt_ref[st], kt_ref[st]

    @pl.when(ki == 0)
    def _():
        m_ref[...] = jnp.full_like(m_ref, -jnp.inf)
        l_ref[...] = jnp.zeros_like(l_ref)
        acc_ref[...] = jnp.zeros_like(acc_ref)

    def update(masked):
        v = v_ref[0]
        for mp in range(2):
            s = _dot_nt(q_ref[0, :, mp * DIFF_HEAD:(mp + 1) * DIFF_HEAD],
                        k_ref[0, :, mp * DIFF_HEAD:(mp + 1) * DIFF_HEAD])
            if masked:
                s = _causal_mask(s, qi, ki, tq, tk)
            m_new, alpha, p = _online_softmax_tiles(s, m_ref[mp], c)
            row_sum = jnp.sum(functools.reduce(jnp.add, p), axis=-1, keepdims=True)
            l_ref[mp] = alpha * l_ref[mp] + row_sum
            pv = jnp.dot(jnp.concatenate(p, axis=-1).astype(BF16), v, preferred_element_type=F32)
            acc_ref[mp] = jnp.concatenate([alpha, alpha], axis=-1) * acc_ref[mp] + pv
            m_ref[mp] = m_new

    @pl.when(ki < qi)
    def _():
        update(False)

    @pl.when(ki == qi)
    def _():
        update(True)
        a0 = acc_ref[0] / jnp.concatenate([l_ref[0], l_ref[0]], axis=-1)
        a1 = acc_ref[1] / jnp.concatenate([l_ref[1], l_ref[1]], axis=-1)
        o = a0 - lam_ref[...] * a1
        o = o * lax.rsqrt(jnp.mean(o * o, axis=-1, keepdims=True) + DIFF_EPS) * g_ref[...]
        o_ref[0] = o.astype(o_ref.dtype)


def diff_flash(q, k, v, lam_row, gain_row, scale, tq=512):
    b, s, hd = q.shape
    w = 2 * DIFF_HEAD
    nh = hd // w
    tq = min(tq, s)
    tk = tq
    qt, kt = _causal_pairs(s // tq)
    q_idx = lambda bi, h, st, qt_ref, kt_ref: (bi, qt_ref[st], h)
    kv_idx = lambda bi, h, st, qt_ref, kt_ref: (bi, kt_ref[st], 0)
    vec = pl.BlockSpec((1, w), lambda bi, h, st, qt_ref, kt_ref: (0, 0))
    grid_spec = pltpu.PrefetchScalarGridSpec(
        num_scalar_prefetch=2, grid=(b, nh, qt.shape[0]),
        in_specs=[pl.BlockSpec((1, tq, w), q_idx), pl.BlockSpec((1, tk, w), kv_idx),
                  pl.BlockSpec((1, tk, w), kv_idx), vec, vec],
        out_specs=pl.BlockSpec((1, tq, w), q_idx),
        scratch_shapes=[pltpu.VMEM((2, tq, LANES), F32), pltpu.VMEM((2, tq, LANES), F32),
                        pltpu.VMEM((2, tq, w), F32)])
    return pl.pallas_call(
        functools.partial(_diff_flash_kernel, c=scale * LOG2E, tq=tq, tk=tk), grid_spec=grid_spec,
        out_shape=jax.ShapeDtypeStruct((b, s, hd), BF16),
        compiler_params=_params("parallel", "parallel", "arbitrary"),
    )(qt, kt, q, k, v, lam_row, gain_row)


def _mla_decode_kernel(pt_ref, ql_ref, qp_ref, cn_ref, kn_ref, *refs, scale, pg):
    ckv_refs, kpe_refs = refs[:pg], refs[pg:2 * pg]
    o_ref, m_ref, l_ref, acc_ref = refs[2 * pg:]
    c = pl.program_id(1)
    ql, qp = ql_ref[0], qp_ref[0]

    @pl.when(c == 0)
    def _():
        cn = cn_ref[0].astype(BF16).astype(F32)
        kn = kn_ref[0].astype(BF16).astype(F32)
        s_new = (jnp.sum(ql.astype(F32) * cn, axis=-1, keepdims=True)
                 + jnp.sum(qp.astype(F32) * kn, axis=-1, keepdims=True)) * scale
        m_ref[...] = s_new
        l_ref[...] = jnp.ones_like(l_ref)
        acc_ref[...] = jnp.broadcast_to(cn, acc_ref.shape)

    pages = [r[0].astype(BF16) for r in ckv_refs]
    s = jnp.concatenate(
        [_dot_nt(ql, pages[g])
         + jnp.dot(qp, kpe_refs[g][0].astype(BF16), preferred_element_type=F32) for g in range(pg)],
        axis=-1) * scale
    m_prev = m_ref[...]
    m_new = jnp.maximum(m_prev, jnp.max(s, axis=-1, keepdims=True))
    alpha = jnp.exp(m_prev - m_new)
    p = jnp.exp(s - m_new)
    l_ref[...] = alpha * l_ref[...] + jnp.sum(p, axis=-1, keepdims=True)
    pb = p.astype(BF16)
    acc = alpha * acc_ref[...]
    for g in range(pg):
        acc = acc + jnp.dot(pb[:, g * PAGE:(g + 1) * PAGE], pages[g], preferred_element_type=F32)
    acc_ref[...] = acc
    m_ref[...] = m_new

    @pl.when(c == pl.num_programs(1) - 1)
    def _():
        o_ref[0] = (acc_ref[...] / l_ref[...]).astype(o_ref.dtype)


def mla_decode(page_table, q_lat, q_pe, ckv_new, kpe_new, cache_ckv, cache_kpe_t, scale, pg=16):
    b, h, c_dim = q_lat.shape
    r_dim = q_pe.shape[2]
    n_pages = page_table.shape[1]
    pg = min(pg, n_pages)
    pt = page_table.reshape(-1)

    def page_idx(g):
        return lambda bi, ci, pt_ref: (pt_ref[bi * n_pages + ci * pg + g], 0, 0)

    per_b = lambda bi, ci, pt_ref: (bi, 0, 0)
    in_specs = [pl.BlockSpec((1, h, c_dim), per_b), pl.BlockSpec((1, h, r_dim), per_b),
                pl.BlockSpec((1, 1, c_dim), per_b), pl.BlockSpec((1, 1, r_dim), per_b)]
    in_specs += [pl.BlockSpec((1, PAGE, c_dim), page_idx(g)) for g in range(pg)]
    in_specs += [pl.BlockSpec((1, r_dim, PAGE), page_idx(g)) for g in range(pg)]
    grid_spec = pltpu.PrefetchScalarGridSpec(
        num_scalar_prefetch=1, grid=(b, n_pages // pg), in_specs=in_specs,
        out_specs=pl.BlockSpec((1, h, c_dim), per_b),
        scratch_shapes=[pltpu.VMEM((h, 1), F32), pltpu.VMEM((h, 1), F32), pltpu.VMEM((h, c_dim), F32)])
    return pl.pallas_call(
        functools.partial(_mla_decode_kernel, scale=scale, pg=pg), grid_spec=grid_spec,
        out_shape=jax.ShapeDtypeStruct((b, h, c_dim), BF16),
        compiler_params=_params("parallel", "arbitrary"),
    )(pt, q_lat, q_pe, ckv_new, kpe_new, *([cache_ckv] * pg), *([cache_kpe_t] * pg))


def _diff_decode_kernel(pt_ref, q_ref, kn_ref, vn_ref, lam_ref, g_ref, *refs, scale, pg):
    k_refs, v_refs = refs[:pg], refs[pg:2 * pg]
    o_ref, m_ref, l_ref, acc_ref = refs[2 * pg:]
    c = pl.program_id(1)

    @pl.when(c == 0)
    def _():
        kn = kn_ref[0].astype(BF16).astype(F32)
        vn = vn_ref[0].astype(BF16).astype(F32)
        for mp in range(2):
            qm = q_ref[0, mp].astype(F32)
            km = kn[:, mp * DIFF_HEAD:(mp + 1) * DIFF_HEAD]
            m_ref[mp] = jnp.sum(qm * km, axis=-1, keepdims=True) * scale
            acc_ref[mp] = jnp.broadcast_to(vn, acc_ref.shape[1:])
        l_ref[...] = jnp.ones_like(l_ref)

    vs = [r[0].astype(BF16) for r in v_refs]
    for mp in range(2):
        qm = q_ref[0, mp]
        s = jnp.concatenate(
            [_dot_nt(qm, k_refs[g][0, pl.ds(mp, PAGE, stride=2), :].astype(BF16)) for g in range(pg)],
            axis=-1) * scale
        m_prev = m_ref[mp]
        m_new = jnp.maximum(m_prev, jnp.max(s, axis=-1, keepdims=True))
        alpha = jnp.exp(m_prev - m_new)
        p = jnp.exp(s - m_new)
        l_ref[mp] = alpha * l_ref[mp] + jnp.sum(p, axis=-1, keepdims=True)
        pb = p.astype(BF16)
        acc = alpha * acc_ref[mp]
        for g in range(pg):
            acc = acc + jnp.dot(pb[:, g * PAGE:(g + 1) * PAGE], vs[g], preferred_element_type=F32)
        acc_ref[mp] = acc
        m_ref[mp] = m_new

    @pl.when(c == pl.num_programs(1) - 1)
    def _():
        o_ref[0] = _diff_finish(acc_ref, l_ref, lam_ref, g_ref).astype(o_ref.dtype)


def diff_decode(page_table, q, k_new, v_new, lam_row, gain_row, cache_k, cache_v, scale, pg=16):
    b, _, h, _ = q.shape
    n_pages = page_table.shape[1]
    pg = min(pg, n_pages)
    pt = page_table.reshape(-1)
    w = 2 * DIFF_HEAD

    def page_idx(g):
        return lambda bi, ci, pt_ref: (pt_ref[bi * n_pages + ci * pg + g], 0, 0)

    per_b3 = lambda bi, ci, pt_ref: (bi, 0, 0)
    vec = pl.BlockSpec((1, w), lambda bi, ci, pt_ref: (0, 0))
    in_specs = [pl.BlockSpec((1, 2, h, DIFF_HEAD), lambda bi, ci, pt_ref: (bi, 0, 0, 0)),
                pl.BlockSpec((1, 1, w), per_b3), pl.BlockSpec((1, 1, w), per_b3), vec, vec]
    in_specs += [pl.BlockSpec((1, 2 * PAGE, DIFF_HEAD), page_idx(g)) for g in range(pg)]
    in_specs += [pl.BlockSpec((1, PAGE, w), page_idx(g)) for g in range(pg)]
    grid_spec = pltpu.PrefetchScalarGridSpec(
        num_scalar_prefetch=1, grid=(b, n_pages // pg), in_specs=in_specs,
        out_specs=pl.BlockSpec((1, h, w), per_b3),
        scratch_shapes=[pltpu.VMEM((2, h, 1), F32), pltpu.VMEM((2, h, 1), F32), pltpu.VMEM((2, h, w), F32)])
    return pl.pallas_call(
        functools.partial(_diff_decode_kernel, scale=scale, pg=pg), grid_spec=grid_spec,
        out_shape=jax.ShapeDtypeStruct((b, h, w), BF16),
        compiler_params=_params("parallel", "arbitrary"),
    )(pt, q, k_new, v_new, lam_row, gain_row, *([cache_k] * pg), *([cache_v] * pg))


def _head_nt_kernel(x_ref, w_ref, o_ref):
    o_ref[...] = _dot_nt(x_ref[...].astype(BF16), w_ref[...].astype(BF16)).astype(o_ref.dtype)


def _head_nn_kernel(x_ref, w_ref, o_ref):
    o_ref[...] = jnp.dot(x_ref[...].astype(BF16), w_ref[...].astype(BF16),
                         preferred_element_type=F32).astype(o_ref.dtype)


def absorb_q(q_nope, w_ukv2):
    b, hd = q_nope.shape
    c_dim = w_ukv2.shape[0]
    nh = hd // QK_NOPE
    return pl.pallas_call(
        _head_nt_kernel, grid=(nh,),
        in_specs=[pl.BlockSpec((b, QK_NOPE), lambda h: (0, h)),
                  pl.BlockSpec((c_dim, QK_NOPE), lambda h: (0, 2 * h))],
        out_specs=pl.BlockSpec((b, c_dim), lambda h: (0, h)),
        out_shape=jax.ShapeDtypeStruct((b, nh * c_dim), BF16),
        compiler_params=_params("parallel"),
    )(q_nope, w_ukv2)


def expand_o(o_lat, w_ukv2):
    b = o_lat.shape[0]
    c_dim = w_ukv2.shape[0]
    nh = o_lat.shape[1] // c_dim
    return pl.pallas_call(
        _head_nn_kernel, grid=(nh,),
        in_specs=[pl.BlockSpec((b, c_dim), lambda h: (0, h)),
                  pl.BlockSpec((c_dim, V_HEAD), lambda h: (0, 2 * h + 1))],
        out_specs=pl.BlockSpec((b, V_HEAD), lambda h: (0, h)),
        out_shape=jax.ShapeDtypeStruct((b, nh * V_HEAD), BF16),
        compiler_params=_params("parallel"),
    )(o_lat, w_ukv2)


def _wkv_kernel(r_ref, w_ref, k_ref, v_ref, a_ref, b_ref, s0_ref, y_ref, st_ref, s_ref, *, ts, n):
    c = pl.program_id(1)

    @pl.when(c == 0)
    def _():
        s_ref[...] = s0_ref[...]

    def row(ref, t, j):
        return ref[t, pl.ds(j, 1), :]

    sa0 = jnp.zeros((n, LANES), F32)
    for j in range(n):
        sa0 = sa0 + s_ref[j] * row(a_ref, 0, j)

    def step(t, sa):
        t_next = jnp.minimum(t + 1, ts - 1)
        v = v_ref[t]
        y = jnp.zeros((n, LANES), F32)
        sa_next = jnp.zeros((n, LANES), F32)
        for j in range(n):
            sj = s_ref[j] * row(w_ref, t, j) + sa * row(b_ref, t, j) + v * row(k_ref, t, j)
            s_ref[j] = sj
            y = y + sj * row(r_ref, t, j)
            sa_next = sa_next + sj * row(a_ref, t_next, j)
        y_ref[t] = y
        return sa_next

    lax.fori_loop(0, ts, step, sa0)

    @pl.when(c == pl.num_programs(1) - 1)
    def _():
        st_ref[...] = s_ref[...]


def wkv_scan(r, w, k, v, a, b, s0, ts=32):
    s, n, l = r.shape
    ts = min(ts, s)
    seq_spec = pl.BlockSpec((ts, n, LANES), lambda g, c: (c, 0, g))
    st_spec = pl.BlockSpec((n, n, LANES), lambda g, c: (0, 0, g))
    return pl.pallas_call(
        functools.partial(_wkv_kernel, ts=ts, n=n),
        grid=(l // LANES, s // ts),
        in_specs=[seq_spec] * 6 + [st_spec],
        out_specs=[seq_spec, st_spec],
        out_shape=[jax.ShapeDtypeStruct((s, n, l), F32), jax.ShapeDtypeStruct((n, n, l), F32)],
        scratch_shapes=[pltpu.VMEM((n, n, LANES), F32)],
        compiler_params=_params("parallel", "arbitrary"),
    )(r, w, k, v, a, b, s0)


def _rope_tables(pos, half):
    inv = ROPE_THETA ** (-jnp.arange(half, dtype=F32) / half)
    ang = pos.astype(F32)[:, None] * inv[None, :]
    return jnp.cos(ang), jnp.sin(ang)


def _rope(x, cos, sin, n_rot):
    half = n_rot // 2
    shape = (x.shape[0],) + (1,) * (x.ndim - 2) + (half,)
    cos, sin = cos.reshape(shape), sin.reshape(shape)
    x1, x2 = x[..., :half], x[..., half:n_rot]
    return jnp.concatenate([x1 * cos - x2 * sin, x2 * cos + x1 * sin, x[..., n_rot:]], axis=-1)


def _rms(x, g, eps):
    return x * lax.rsqrt(jnp.mean(jnp.square(x), axis=-1, keepdims=True) + eps) * g


class _Tokens:
    def __init__(self, batch, seq, dec_batch, past_len):
        self.batch, self.seq, self.dec_batch, self.past_len = batch, seq, dec_batch, past_len
        self.n_prompt = batch * seq
        self.n_tokens = self.n_prompt + dec_batch
        self.pos = jnp.concatenate([jnp.tile(jnp.arange(seq, dtype=jnp.int32), batch),
                                    jnp.full((dec_batch,), past_len, jnp.int32)])


def _mla_mixer(tk, h, p, cache_ckv, cache_kpe, page_table):
    n_p, nb, seq, ns = tk.n_prompt, tk.batch, tk.seq, tk.dec_batch
    c_dim = p['w_dkv'].shape[1] - QK_ROPE
    nh = p['w_ukv'].shape[1]
    qd = QK_NOPE + QK_ROPE
    scale = qd ** -0.5
    cos, sin = _rope_tables(tk.pos, QK_ROPE // 2)

    cq = _rms(matmul(h, p['w_dq']), p['q_norm'], MLA_EPS).astype(BF16)
    w_uq = p['w_uq'].reshape(-1, nh, qd)
    w_uq = jnp.concatenate([w_uq[..., :QK_NOPE].reshape(-1, nh * QK_NOPE),
                            w_uq[..., QK_NOPE:].reshape(-1, nh * QK_ROPE)], axis=1)
    q = matmul(cq, w_uq)
    q_nope = q[:, :nh * QK_NOPE].astype(BF16)
    q_pe = _rope(q[:, nh * QK_NOPE:].reshape(-1, nh, QK_ROPE), cos, sin, QK_ROPE).astype(BF16)
    q_pe = q_pe.reshape(-1, nh * QK_ROPE)

    kv = matmul(h, p['w_dkv'])
    ckv = _rms(kv[:, :c_dim], p['kv_norm'], MLA_EPS)
    kpe = _rope(kv[:, c_dim:], cos, sin, QK_ROPE)
    w_ukv2 = p['w_ukv'].reshape(c_dim, nh * (QK_NOPE + V_HEAD))

    kvb = matmul(ckv[:n_p].astype(BF16), w_ukv2, out_dtype=BF16)
    kpe_b = kpe[:n_p].astype(BF16)
    zeros = jnp.zeros_like(kpe_b)
    kpe2 = jnp.concatenate([kpe_b, zeros, zeros, kpe_b], axis=-1)
    o_p = mla_flash(q_nope[:n_p].reshape(nb, seq, -1), q_pe[:n_p].reshape(nb, seq, -1),
                    kvb.reshape(nb, seq, -1), kpe2.reshape(nb, seq, -1), scale)

    q_lat = absorb_q(q_nope[n_p:], w_ukv2).reshape(ns, nh, c_dim)
    o_lat = mla_decode(page_table, q_lat, q_pe[n_p:].reshape(ns, nh, QK_ROPE),
                       ckv[n_p:].reshape(ns, 1, c_dim), kpe[n_p:].reshape(ns, 1, QK_ROPE),
                       cache_ckv, jnp.swapaxes(cache_kpe, 1, 2), scale)
    o_s = expand_o(o_lat.reshape(ns, nh * c_dim), w_ukv2)

    o = jnp.concatenate([o_p.reshape(n_p, -1), o_s], axis=0)
    out = matmul(o, p['w_o'])
    return out, (ckv[:n_p].reshape(nb, seq, c_dim), kpe[:n_p].reshape(nb, seq, QK_ROPE),
                 ckv[n_p:].reshape(ns, 1, c_dim), kpe[n_p:].reshape(ns, 1, QK_ROPE))


def _diff_mixer(tk, h, p, cache_k, cache_v, page_table, layer_idx):
    n_p, nb, seq, ns = tk.n_prompt, tk.batch, tk.seq, tk.dec_batch
    w = 2 * DIFF_HEAD
    nh = p['w_q'].shape[1] // w
    scale = DIFF_HEAD ** -0.5
    cos, sin = _rope_tables(tk.pos, DIFF_ROT // 2)
    lam_init = 0.8 - 0.6 * math.exp(-0.3 * layer_idx)
    lam = (jnp.exp(jnp.sum(p['lam_q1'] * p['lam_k1'])) - jnp.exp(jnp.sum(p['lam_q2'] * p['lam_k2'])) + lam_init)
    lam_row = jnp.full((1, w), lam, F32)
    gain_row = (p['subln'] * (1.0 - lam_init)).reshape(1, w)

    q = _rope(matmul(h, p['w_q']).reshape(-1, nh, 2, DIFF_HEAD), cos, sin, DIFF_ROT).astype(BF16)
    k = _rope(matmul(h, p['w_k']).reshape(-1, 2, DIFF_HEAD), cos, sin, DIFF_ROT)
    v = matmul(h, p['w_v'])
    k2 = k.reshape(-1, w)

    o_p = diff_flash(q[:n_p].reshape(nb, seq, nh * w), k2[:n_p].astype(BF16).reshape(nb, seq, w),
                     v[:n_p].astype(BF16).reshape(nb, seq, w), lam_row, gain_row, scale)
    q_s = jnp.swapaxes(q[n_p:], 1, 2)
    o_s = diff_decode(page_table, q_s, k2[n_p:].reshape(ns, 1, w), v[n_p:].reshape(ns, 1, w),
                      lam_row, gain_row, cache_k.reshape(-1, 2 * PAGE, DIFF_HEAD), cache_v, scale)
    o = jnp.concatenate([o_p.reshape(n_p, -1), o_s.reshape(ns, -1)], axis=0)
    out = matmul(o, p['w_o'])
    return out, (k[:n_p].reshape(nb, seq, 2, DIFF_HEAD), v[:n_p].reshape(nb, seq, w),
                 k[n_p:].reshape(ns, 1, 2, DIFF_HEAD), v[n_p:].reshape(ns, 1, w))


def _rwkv_mixer(tk, h, p, shift0, wkv0):
    n_p, nb, seq, ns = tk.n_prompt, tk.batch, tk.seq, tk.dec_batch
    d = h.shape[1]
    n = RWKV_HEAD
    nh = d // n
    hp = h[:n_p].reshape(nb, seq, d)
    prev = jnp.concatenate([jnp.zeros((nb, 1, d), F32), hp[:, :-1]], axis=1).reshape(n_p, d)
    prev = jnp.concatenate([prev, shift0], axis=0)
    xx = prev - h
    xr, xw, xk, xv, xa, xg = ((h + xx * p['mix'][j]).astype(BF16) for j in range(6))
    r = matmul(xr, p['w_r'])
    k = matmul(xk, p['w_k'])
    v = matmul(xv, p['w_v'])
    lw = matmul(matmul(xw, p['w1'], act="tanh", out_dtype=BF16), p['w2'], bias=p['w0'])
    log_w = -jax.nn.softplus(-lw) - 0.5
    decay = jnp.exp(-jnp.exp(log_w))
    a = matmul(matmul(xa, p['a1'], out_dtype=BF16), p['a2'], bias=p['a0'], act="sigmoid")
    g = matmul(matmul(xg, p['g1'], act="sigmoid", out_dtype=BF16), p['g2'])
    kk = (k * p['k_k']).reshape(-1, nh, n)
    kk = kk / jnp.maximum(jnp.sqrt(jnp.sum(jnp.square(kk), axis=-1, keepdims=True)), 1e-12)
    k = k * (1.0 + (a - 1.0) * p['k_a'])
    aa = -kk
    bb = kk * a.reshape(-1, nh, n)

    def lanes_p(t):
        return t[:n_p].reshape(nb, seq, nh, n).transpose(1, 3, 0, 2).reshape(seq, n, nb * nh)

    def lanes_s(t):
        return t[n_p:].reshape(ns, nh, n).transpose(2, 0, 1).reshape(1, n, ns * nh)

    seqs = (r, decay, k, v, aa.reshape(-1, d), bb.reshape(-1, d))
    y_p, st_p = wkv_scan(*(lanes_p(t) for t in seqs), jnp.zeros((n, n, nb * nh), F32))
    s0_s = wkv0.transpose(3, 2, 0, 1).reshape(n, n, ns * nh)
    y_s, st_s = wkv_scan(*(lanes_s(t) for t in seqs), s0_s)
    y = jnp.concatenate([y_p.reshape(seq, n, nb, nh).transpose(2, 0, 3, 1).reshape(n_p, nh, n),
                         y_s.reshape(n, ns, nh).transpose(1, 2, 0)], axis=0)
    wkv_p = st_p.reshape(n, n, nb, nh).transpose(2, 3, 1, 0)
    wkv_s = st_s.reshape(n, n, ns, nh).transpose(2, 3, 1, 0)

    mu = jnp.mean(y, axis=-1, keepdims=True)
    var = jnp.mean(jnp.square(y - mu), axis=-1, keepdims=True)
    yn = ((y - mu) * lax.rsqrt(var + RWKV_GN_EPS)).reshape(-1, d) * p['lnx_g'] + p['lnx_b']
    bonus = jnp.sum((r * k).reshape(-1, nh, n) * p['r_k'], axis=-1, keepdims=True) * v.reshape(-1, nh, n)
    yo = ((yn + bonus.reshape(-1, d)) * g).astype(BF16)
    out = matmul(yo, p['w_o'])
    return out, (hp[:, -1], wkv_p, h[n_p:], wkv_s)


def _dense_ffn(h, w1, w3, w2):
    return matmul(swiglu_up(h, w1, w3), w2.astype(BF16), tm_cap=416, split_k=False)


def _moe_ffn(h, router, w1, w3, w2, tm=512):
    t, d = h.shape
    n_exp = router.shape[1]
    logits = matmul(h, router)
    top_v, top_i = lax.top_k(logits, TOP_K)
    gates = jax.nn.softmax(top_v, axis=-1)

    n_assign = t * TOP_K
    n_tiles = pl.cdiv(n_assign, tm) + n_exp
    n_slots = n_tiles * tm
    flat_e = top_i.reshape(-1).astype(jnp.int32)
    order = jnp.argsort(flat_e, stable=True).astype(jnp.int32)
    sorted_e = flat_e[order]
    counts = jnp.zeros((n_exp,), jnp.int32).at[flat_e].add(1)
    padded = ((counts + tm - 1) // tm) * tm
    pad_end = jnp.cumsum(padded)
    pad_start = pad_end - padded
    start = jnp.cumsum(counts) - counts
    dest = pad_start[sorted_e] + (jnp.arange(n_assign, dtype=jnp.int32) - start[sorted_e])
    slot_tok = jnp.zeros((n_slots,), jnp.int32).at[dest].set(order // TOP_K)
    slot_gate = jnp.zeros((n_slots,), F32).at[dest].set(gates.reshape(-1)[order])
    slot_of = jnp.zeros((n_assign,), jnp.int32).at[order].set(dest)
    tile_start = jnp.arange(n_tiles, dtype=jnp.int32) * tm
    tile_expert = jnp.minimum(jnp.searchsorted(pad_end, tile_start, side='right'), n_exp - 1).astype(jnp.int32)
    n_used = (pad_end[-1:] // tm).astype(jnp.int32)

    xg = gather_rows(slot_tok, h, BF16)
    mid = grouped_swiglu_up(tile_expert, n_used, xg, w1, w3, tm)
    y = grouped_down(tile_expert, n_used, mid, w2, slot_gate.reshape(n_slots, 1), tm)
    return combine_pairs(slot_of, y)


def _forward(x_prompt, x_sample, page_table, c_prompt, c_sample, layers):
    nb, seq, d = x_prompt.shape
    ns = x_sample.shape[0]
    n_pages = page_table.shape[1]
    tk = _Tokens(nb, seq, ns, n_pages * PAGE)
    depth = len(layers)
    alpha = (2 * depth) ** 0.25
    x = jnp.concatenate([x_prompt.reshape(nb * seq, d), x_sample.reshape(ns, d)], axis=0)

    pad = (-(ns + nb)) % 16
    c_all = jax.nn.silu(jnp.concatenate([c_sample, c_prompt, jnp.zeros((pad, d), F32)], axis=0)).astype(BF16)
    mods = []
    for L in layers:
        mod = matmul(c_all, L['ada'][0], bias=L['ada'][1])
        chunks = []
        for j in range(6):
            col = mod[:, j * d:(j + 1) * d]
            chunks.append((col[ns:ns + nb].reshape(nb, 1, d), col[:ns]))
        mods.append(chunks)

    nm = functools.partial(norm_mod, n_prompt=tk.n_prompt, seq=seq)
    outs = []
    h_dtypes = lambda i: (BF16, F32) if layers[i]['kind'] == 'rwkv7' else (BF16,)
    hs = nm(x, scale=mods[0][1], shift=mods[0][0], h_dtypes=h_dtypes(0))
    for i, L in enumerate(layers):
        m = mods[i]
        kind = L['kind']
        if kind == 'mla':
            o, new = _mla_mixer(tk, hs[0], L['mix'], L['state'][0], L['state'][1], page_table)
        elif kind == 'diff':
            o, new = _diff_mixer(tk, hs[0], L['mix'], L['state'][0], L['state'][1], page_table, i)
        else:
            o, new = _rwkv_mixer(tk, hs[1], L['mix'], L['state'][0], L['state'][1])
        outs.append(new)
        dense = len(L['ffn']) == 3
        x, h2 = nm(x, sub=o, gate=m[2], ln=L['ln'][:2], scale=m[4], shift=m[3], alpha=alpha,
                   h_dtypes=(BF16,) if dense else (F32,))
        if dense:
            f = _dense_ffn(h2, *L['ffn'])
        else:
            f = _moe_ffn(h2, *L['ffn'])
        if i + 1 < depth:
            res = nm(x, sub=f, gate=m[5], ln=L['ln'][2:], scale=mods[i + 1][1], shift=mods[i + 1][0],
                     h_dtypes=h_dtypes(i + 1), alpha=alpha)
            x, hs = res[0], res[1:]
        else:
            x, = nm(x, sub=f, gate=m[5], ln=L['ln'][2:], alpha=alpha)
    y_prompt = x[:tk.n_prompt].reshape(nb, seq, d)
    y_sample = x[tk.n_prompt:].reshape(ns, 1, d)
    flat = [y_prompt, y_sample]
    for new in outs:
        flat.extend(new)
    return tuple(flat)


def kernel(x_prompt, x_sample, cache_l0_ckv, cache_l0_kpe, cache_l1_k, cache_l1_v, state_l2_shift, state_l2_wkv, cache_l3_ckv, cache_l3_kpe, page_table, c_prompt, c_sample, l0_ada_w, l0_ada_b, l0_ln1_g, l0_ln1_b, l0_ln2_g, l0_ln2_b, l0_w_dq, l0_q_norm, l0_w_uq, l0_w_dkv, l0_kv_norm, l0_w_ukv, l0_w_o, l0_ffn_w1, l0_ffn_w3, l0_ffn_w2, l1_ada_w, l1_ada_b, l1_ln1_g, l1_ln1_b, l1_ln2_g, l1_ln2_b, l1_w_q, l1_w_k, l1_w_v, l1_lam_q1, l1_lam_k1, l1_lam_q2, l1_lam_k2, l1_subln, l1_w_o, l1_router, l1_exp_w1, l1_exp_w3, l1_exp_w2, l2_ada_w, l2_ada_b, l2_ln1_g, l2_ln1_b, l2_ln2_g, l2_ln2_b, l2_mix, l2_w_r, l2_w_k, l2_w_v, l2_w_o, l2_w0, l2_w1, l2_w2, l2_a0, l2_a1, l2_a2, l2_g1, l2_g2, l2_k_k, l2_k_a, l2_r_k, l2_lnx_g, l2_lnx_b, l2_ffn_w1, l2_ffn_w3, l2_ffn_w2, l3_ada_w, l3_ada_b, l3_ln1_g, l3_ln1_b, l3_ln2_g, l3_ln2_b, l3_w_dq, l3_q_norm, l3_w_uq, l3_w_dkv, l3_kv_norm, l3_w_ukv, l3_w_o, l3_router, l3_exp_w1, l3_exp_w3, l3_exp_w2):
    layers = (
        dict(kind='mla', ada=(l0_ada_w, l0_ada_b), ln=(l0_ln1_g, l0_ln1_b, l0_ln2_g, l0_ln2_b),
             mix=dict(w_dq=l0_w_dq, q_norm=l0_q_norm, w_uq=l0_w_uq, w_dkv=l0_w_dkv, kv_norm=l0_kv_norm,
                      w_ukv=l0_w_ukv, w_o=l0_w_o),
             state=(cache_l0_ckv, cache_l0_kpe), ffn=(l0_ffn_w1, l0_ffn_w3, l0_ffn_w2)),
        dict(kind='diff', ada=(l1_ada_w, l1_ada_b), ln=(l1_ln1_g, l1_ln1_b, l1_ln2_g, l1_ln2_b),
             mix=dict(w_q=l1_w_q, w_k=l1_w_k, w_v=l1_w_v, lam_q1=l1_lam_q1, lam_k1=l1_lam_k1,
                      lam_q2=l1_lam_q2, lam_k2=l1_lam_k2, subln=l1_subln, w_o=l1_w_o),
             state=(cache_l1_k, cache_l1_v), ffn=(l1_router, l1_exp_w1, l1_exp_w3, l1_exp_w2)),
        dict(kind='rwkv7', ada=(l2_ada_w, l2_ada_b), ln=(l2_ln1_g, l2_ln1_b, l2_ln2_g, l2_ln2_b),
             mix=dict(mix=l2_mix, w_r=l2_w_r, w_k=l2_w_k, w_v=l2_w_v, w_o=l2_w_o, w0=l2_w0, w1=l2_w1, w2=l2_w2,
                      a0=l2_a0, a1=l2_a1, a2=l2_a2, g1=l2_g1, g2=l2_g2, k_k=l2_k_k, k_a=l2_k_a, r_k=l2_r_k,
                      lnx_g=l2_lnx_g, lnx_b=l2_lnx_b),
             state=(state_l2_shift, state_l2_wkv), ffn=(l2_ffn_w1, l2_ffn_w3, l2_ffn_w2)),
        dict(kind='mla', ada=(l3_ada_w, l3_ada_b), ln=(l3_ln1_g, l3_ln1_b, l3_ln2_g, l3_ln2_b),
             mix=dict(w_dq=l3_w_dq, q_norm=l3_q_norm, w_uq=l3_w_uq, w_dkv=l3_w_dkv, kv_norm=l3_kv_norm,
                      w_ukv=l3_w_ukv, w_o=l3_w_o),
             state=(cache_l3_ckv, cache_l3_kpe), ffn=(l3_router, l3_exp_w1, l3_exp_w3, l3_exp_w2)),
    )
    return _forward(x_prompt, x_sample, page_table, c_prompt, c_sample, layers)
```

```python
import functools
import math

import jax
import jax.numpy as jnp
from jax import lax
from jax.experimental import pallas as pl
from jax.experimental.pallas import tpu as pltpu

F32 = jnp.float32
BF16 = jnp.bfloat16

LANES = 128
VMEM_LIMIT_BYTES = 56 * 1024 * 1024

ROPE_THETA = 500000.0
LN_EPS = 1e-5
MLA_EPS = 1e-6
DIFF_EPS = 1e-5
RWKV_GN_EPS = 64e-5
QK_NOPE = 128
QK_ROPE = 64
V_HEAD = 128
DIFF_HEAD = 128
DIFF_ROT = DIFF_HEAD // 4
RWKV_HEAD = 64
TOP_K = 2
PAGE = 128


def _params(*sem):
    return pltpu.CompilerParams(dimension_semantics=sem, vmem_limit_bytes=VMEM_LIMIT_BYTES)


def _row_tile(m, cap):
    if m <= cap:
        return m
    for t in range(cap - cap % 16, 15, -16):
        if m % t == 0:
            return t
    return cap


def _act(v, act):
    if act == "tanh":
        return jnp.tanh(v)
    if act == "sigmoid":
        return jax.nn.sigmoid(v)
    return v


def _matmul_kernel(*refs, nk, has_bias, act):
    if has_bias:
        x_ref, w_ref, b_ref, o_ref = refs
    else:
        x_ref, w_ref, o_ref = refs
        b_ref = None
    acc = jnp.dot(x_ref[...].astype(BF16), w_ref[...].astype(BF16), preferred_element_type=F32)

    def finish(v):
        if has_bias:
            v = v + b_ref[...]
        return _act(v, act).astype(o_ref.dtype)

    if nk == 1:
        o_ref[...] = finish(acc)
    else:
        k = pl.program_id(2)

        @pl.when(k == 0)
        def _():
            o_ref[...] = acc

        @pl.when(jnp.logical_and(k > 0, k < nk - 1))
        def _():
            o_ref[...] += acc

        @pl.when(k == nk - 1)
        def _():
            o_ref[...] = finish(o_ref[...] + acc)


def _k_split(k):
    if k <= 4096:
        return 1
    for nk in range(2, 65):
        if k % nk == 0 and (k // nk) % LANES == 0 and k // nk <= 5504:
            return nk
    raise ValueError(f"no K split for {k}")


def matmul(x, w, bias=None, act=None, out_dtype=F32, tm_cap=1024, split_k=True):
    m, k = x.shape
    n = w.shape[1]
    nk = _k_split(k) if split_k else 1
    tk = k // nk
    tm = _row_tile(m, tm_cap)
    if nk > 1:
        tn = 256
        assert out_dtype == F32
    else:
        tn = n if n <= 640 else 512
    grid = (pl.cdiv(m, tm), pl.cdiv(n, tn), nk)
    in_specs = [pl.BlockSpec((tm, tk), lambda i, j, kk: (i, kk)),
                pl.BlockSpec((tk, tn), lambda i, j, kk: (kk, j))]
    args = [x, w]
    if bias is not None:
        in_specs.append(pl.BlockSpec((1, tn), lambda i, j, kk: (0, j)))
        args.append(bias.reshape(1, n).astype(F32))
    return pl.pallas_call(
        functools.partial(_matmul_kernel, nk=nk, has_bias=bias is not None, act=act),
        grid=grid, in_specs=in_specs,
        out_specs=pl.BlockSpec((tm, tn), lambda i, j, kk: (i, j)),
        out_shape=jax.ShapeDtypeStruct((m, n), out_dtype),
        compiler_params=_params("parallel", "parallel", "arbitrary"),
    )(*args)


def _swiglu_kernel(x_ref, w1_ref, w3_ref, o_ref):
    x = x_ref[...].astype(BF16)
    a = jnp.dot(x, w1_ref[...].astype(BF16), preferred_element_type=F32)
    b = jnp.dot(x, w3_ref[...].astype(BF16), preferred_element_type=F32)
    o_ref[...] = (a * jax.nn.sigmoid(a) * b).astype(o_ref.dtype)


def swiglu_up(x, w1, w3, tm_cap=1024, tn=256):
    m, k = x.shape
    n = w1.shape[1]
    tm = _row_tile(m, tm_cap)
    return pl.pallas_call(
        _swiglu_kernel,
        grid=(pl.cdiv(m, tm), pl.cdiv(n, tn)),
        in_specs=[pl.BlockSpec((tm, k), lambda i, j: (i, 0)),
                  pl.BlockSpec((k, tn), lambda i, j: (0, j)),
                  pl.BlockSpec((k, tn), lambda i, j: (0, j))],
        out_specs=pl.BlockSpec((tm, tn), lambda i, j: (i, j)),
        out_shape=jax.ShapeDtypeStruct((m, n), BF16),
        compiler_params=_params("parallel", "parallel"),
    )(x, w1, w3)


def _grouped_swiglu_kernel(te_ref, nu_ref, x_ref, w1_ref, w3_ref, o_ref):
    i = pl.program_id(1)

    @pl.when(i < nu_ref[0])
    def _():
        x = x_ref[...]
        a = jnp.dot(x, w1_ref[...].astype(BF16), preferred_element_type=F32)
        b = jnp.dot(x, w3_ref[...].astype(BF16), preferred_element_type=F32)
        o_ref[...] = (a * jax.nn.sigmoid(a) * b).astype(o_ref.dtype)

    @pl.when(i >= nu_ref[0])
    def _():
        o_ref[...] = jnp.zeros_like(o_ref)


def grouped_swiglu_up(tile_expert, n_used, xg, w1, w3, tm, tn=256):
    p, k = xg.shape
    n = w1.shape[2]
    grid_spec = pltpu.PrefetchScalarGridSpec(
        num_scalar_prefetch=2, grid=(n // tn, p // tm),
        in_specs=[pl.BlockSpec((tm, k), lambda j, i, te, nu: (i, 0)),
                  pl.BlockSpec((None, k, tn), lambda j, i, te, nu: (te[i], 0, j)),
                  pl.BlockSpec((None, k, tn), lambda j, i, te, nu: (te[i], 0, j))],
        out_specs=pl.BlockSpec((tm, tn), lambda j, i, te, nu: (i, j)))
    return pl.pallas_call(
        _grouped_swiglu_kernel, grid_spec=grid_spec,
        out_shape=jax.ShapeDtypeStruct((p, n), BF16),
        compiler_params=_params("parallel", "arbitrary"),
    )(tile_expert, n_used, xg, w1, w3)


def _grouped_down_kernel(te_ref, nu_ref, h_ref, w_ref, g_ref, o_ref):
    i = pl.program_id(1)

    @pl.when(i < nu_ref[0])
    def _():
        y = jnp.dot(h_ref[...], w_ref[...].astype(BF16), preferred_element_type=F32)
        o_ref[...] = y * g_ref[...]

    @pl.when(i >= nu_ref[0])
    def _():
        o_ref[...] = jnp.zeros_like(o_ref)


def grouped_down(tile_expert, n_used, h, w2, gate_col, tm, tn=512):
    p, k = h.shape
    n = w2.shape[2]
    grid_spec = pltpu.PrefetchScalarGridSpec(
        num_scalar_prefetch=2, grid=(n // tn, p // tm),
        in_specs=[pl.BlockSpec((tm, k), lambda j, i, te, nu: (i, 0)),
                  pl.BlockSpec((None, k, tn), lambda j, i, te, nu: (te[i], 0, j)),
                  pl.BlockSpec((tm, 1), lambda j, i, te, nu: (i, 0))],
        out_specs=pl.BlockSpec((tm, tn), lambda j, i, te, nu: (i, j)))
    return pl.pallas_call(
        _grouped_down_kernel, grid_spec=grid_spec,
        out_shape=jax.ShapeDtypeStruct((p, n), F32),
        compiler_params=_params("parallel", "arbitrary"),
    )(tile_expert, n_used, h, w2, gate_col)


def _gather_rows_kernel(idx_ref, x_hbm, o_ref, buf, sem, *, rows):
    base = pl.program_id(0) * rows

    def row_copy(r, src):
        return pltpu.make_async_copy(x_hbm.at[pl.ds(src, 1)], buf.at[pl.ds(r, 1)], sem)

    def issue(r, carry):
        row_copy(r, idx_ref[base + r]).start()
        return carry

    def drain(r, carry):
        row_copy(r, 0).wait()
        return carry

    lax.fori_loop(0, rows, issue, 0)
    lax.fori_loop(0, rows, drain, 0)
    o_ref[...] = buf[...].astype(o_ref.dtype)


def gather_rows(idx, x, out_dtype, rows=128):
    p = idx.shape[0]
    c = x.shape[1]
    rows = _row_tile(p, rows)
    grid_spec = pltpu.PrefetchScalarGridSpec(
        num_scalar_prefetch=1, grid=(p // rows,),
        in_specs=[pl.BlockSpec(memory_space=pl.ANY)],
        out_specs=pl.BlockSpec((rows, c), lambda i, idx_ref: (i, 0)),
        scratch_shapes=[pltpu.VMEM((rows, c), x.dtype), pltpu.SemaphoreType.DMA(())])
    return pl.pallas_call(
        functools.partial(_gather_rows_kernel, rows=rows), grid_spec=grid_spec,
        out_shape=jax.ShapeDtypeStruct((p, c), out_dtype),
        compiler_params=_params("arbitrary"),
    )(idx, x)


def _combine_kernel(idx_ref, y_hbm, o_ref, buf0, buf1, sem, *, rows):
    base = pl.program_id(0) * rows

    def row_copy(r, src, buf):
        return pltpu.make_async_copy(y_hbm.at[pl.ds(src, 1)], buf.at[pl.ds(r, 1)], sem)

    def issue(r, carry):
        row_copy(r, idx_ref[2 * (base + r)], buf0).start()
        row_copy(r, idx_ref[2 * (base + r) + 1], buf1).start()
        return carry

    def drain(r, carry):
        row_copy(r, 0, buf0).wait()
        row_copy(r, 0, buf1).wait()
        return carry

    lax.fori_loop(0, rows, issue, 0)
    lax.fori_loop(0, rows, drain, 0)
    o_ref[...] = buf0[...] + buf1[...]


def combine_pairs(idx2, y, rows=128):
    t = idx2.shape[0] // 2
    c = y.shape[1]
    rows = _row_tile(t, rows)
    grid_spec = pltpu.PrefetchScalarGridSpec(
        num_scalar_prefetch=1, grid=(t // rows,),
        in_specs=[pl.BlockSpec(memory_space=pl.ANY)],
        out_specs=pl.BlockSpec((rows, c), lambda i, idx_ref: (i, 0)),
        scratch_shapes=[pltpu.VMEM((rows, c), F32), pltpu.VMEM((rows, c), F32),
                        pltpu.SemaphoreType.DMA(())])
    return pl.pallas_call(
        functools.partial(_combine_kernel, rows=rows), grid_spec=grid_spec,
        out_shape=jax.ShapeDtypeStruct((t, c), F32),
        compiler_params=_params("arbitrary"),
    )(idx2, y)


def _norm_mod_kernel(*refs, alpha, do_norm, do_mod, n_prompt_tiles, rows_per_batch_tiles):
    it = iter(refs)
    x_ref = next(it)
    is_sample = pl.program_id(0) >= n_prompt_tiles

    def pick(p_ref, s_ref):
        return jnp.where(is_sample, s_ref[...], p_ref[0])

    if do_norm:
        sub_ref, gp_ref, gs_ref, g_ref, b_ref = (next(it) for _ in range(5))
    if do_mod:
        scp_ref, scs_ref, shp_ref, shs_ref = (next(it) for _ in range(4))
    outs = list(it)
    x = x_ref[...]
    if do_norm:
        z = alpha * x + (1.0 + pick(gp_ref, gs_ref)) * sub_ref[...]
        mu = jnp.mean(z, axis=-1, keepdims=True)
        zc = z - mu
        var = jnp.mean(zc * zc, axis=-1, keepdims=True)
        x = zc * lax.rsqrt(var + LN_EPS) * g_ref[...] + b_ref[...]
        outs.pop(0)[...] = x
    if do_mod:
        h = x * (1.0 + pick(scp_ref, scs_ref)) + pick(shp_ref, shs_ref)
        for o_ref in outs:
            o_ref[...] = h.astype(o_ref.dtype)


def norm_mod(x, *, n_prompt, seq, sub=None, gate=None, ln=None, scale=None, shift=None,
             h_dtypes=(BF16,), alpha=1.0, rows=128):
    nt, d = x.shape
    ns = nt - n_prompt
    rows = min(rows, ns)
    assert ns % rows == 0 and seq % rows == 0
    n_prompt_tiles = n_prompt // rows
    tiles_per_batch = seq // rows
    nb = n_prompt // seq
    row_spec = pl.BlockSpec((rows, d), lambda i: (i, 0))
    pb_spec = pl.BlockSpec((1, 1, d), lambda i: (jnp.minimum(i // tiles_per_batch, nb - 1), 0, 0))
    ps_spec = pl.BlockSpec((rows, d), lambda i: (jnp.maximum(i - n_prompt_tiles, 0), 0))
    vec_spec = pl.BlockSpec((1, d), lambda i: (0, 0))
    args, in_specs = [x], [row_spec]
    do_norm, do_mod = sub is not None, scale is not None
    out_shape, out_specs = [], []
    if do_norm:
        args += [sub, gate[0], gate[1], ln[0].reshape(1, d), ln[1].reshape(1, d)]
        in_specs += [row_spec, pb_spec, ps_spec, vec_spec, vec_spec]
        out_shape.append(jax.ShapeDtypeStruct((nt, d), F32))
        out_specs.append(row_spec)
    if do_mod:
        args += [scale[0], scale[1], shift[0], shift[1]]
        in_specs += [pb_spec, ps_spec, pb_spec, ps_spec]
        for dt in h_dtypes:
            out_shape.append(jax.ShapeDtypeStruct((nt, d), dt))
            out_specs.append(row_spec)
    return pl.pallas_call(
        functools.partial(_norm_mod_kernel, alpha=alpha, do_norm=do_norm, do_mod=do_mod,
                          n_prompt_tiles=n_prompt_tiles, rows_per_batch_tiles=tiles_per_batch),
        grid=(nt // rows,), in_specs=in_specs, out_specs=out_specs, out_shape=out_shape,
        compiler_params=_params("parallel"),
    )(*args)


LOG2E = math.log2(math.e)


def _causal_mask(s, qi, ki, tq, tk):
    row = qi * tq + lax.broadcasted_iota(jnp.int32, s.shape, 0)
    col = ki * tk + lax.broadcasted_iota(jnp.int32, s.shape, 1)
    return jnp.where(row >= col, s, -jnp.inf)


def _dot_nt(a, b):
    return lax.dot_general(a, b, (((1,), (1,)), ((), ())), preferred_element_type=F32)


def _lane_tiles(s):
    return [s[:, j * LANES:(j + 1) * LANES] for j in range(s.shape[1] // LANES)]


def _online_softmax_tiles(s, m_prev, c):
    tiles = _lane_tiles(s)
    mx = functools.reduce(jnp.maximum, tiles)
    m_new = jnp.maximum(m_prev, jnp.max(mx, axis=-1, keepdims=True))
    alpha = jnp.exp2((m_prev - m_new) * c)
    return m_new, alpha, [jnp.exp2((t - m_new) * c) for t in tiles]


def _causal_pairs(n):
    pairs = [(qi, ki) for qi in range(n) for ki in range(qi + 1)]
    return (jnp.asarray([p[0] for p in pairs], jnp.int32), jnp.asarray([p[1] for p in pairs], jnp.int32))


def _mla_flash_kernel(qt_ref, kt_ref, qn_ref, qp_ref, kv_ref, kpe_ref, o_ref, m_ref, acc_ref, *, c, tq, tk):
    st = pl.program_id(2)
    qi, ki = qt_ref[st], kt_ref[st]

    @pl.when(ki == 0)
    def _():
        m_ref[...] = jnp.full_like(m_ref, -jnp.inf)
        acc_ref[...] = jnp.zeros_like(acc_ref)

    def update(masked):
        qp = qp_ref[0]
        ones = jnp.ones((tk, V_HEAD), BF16)
        for h in range(2):
            q = jnp.concatenate([qn_ref[0, :, h * QK_NOPE:(h + 1) * QK_NOPE], qp], axis=-1)
            k = jnp.concatenate([kv_ref[0, :, h * 256:h * 256 + QK_NOPE],
                                 kpe_ref[0, :, h * LANES:(h + 1) * LANES]], axis=-1)
            v1 = jnp.concatenate([kv_ref[0, :, h * 256 + QK_NOPE:(h + 1) * 256], ones], axis=-1)
            s = _dot_nt(q, k)
            if masked:
                s = _causal_mask(s, qi, ki, tq, tk)
            m_new, alpha, p = _online_softmax_tiles(s, m_ref[h], c)
            pv = jnp.dot(jnp.concatenate(p, axis=-1).astype(BF16), v1, preferred_element_type=F32)
            acc_ref[h] = jnp.concatenate([alpha, alpha], axis=-1) * acc_ref[h] + pv
            m_ref[h] = m_new

    @pl.when(ki < qi)
    def _():
        update(False)

    @pl.when(ki == qi)
    def _():
        update(True)
        for h in range(2):
            acc = acc_ref[h]
            o_ref[0, :, h * V_HEAD:(h + 1) * V_HEAD] = (acc[:, :V_HEAD] / acc[:, V_HEAD:]).astype(o_ref.dtype)


def mla_flash(qn, qp, kvb, kpe2, scale, tq=512):
    b, s, hd = qn.shape
    hp = hd // (2 * QK_NOPE)
    tq = min(tq, s)
    tk = tq
    qt, kt = _causal_pairs(s // tq)
    q_idx = lambda bi, h, st, qt_ref, kt_ref: (bi, qt_ref[st], h)
    grid_spec = pltpu.PrefetchScalarGridSpec(
        num_scalar_prefetch=2, grid=(b, hp, qt.shape[0]),
        in_specs=[pl.BlockSpec((1, tq, 2 * QK_NOPE), q_idx),
                  pl.BlockSpec((1, tq, 2 * QK_ROPE), q_idx),
                  pl.BlockSpec((1, tk, 512), lambda bi, h, st, qt_ref, kt_ref: (bi, kt_ref[st], h)),
                  pl.BlockSpec((1, tk, 256), lambda bi, h, st, qt_ref, kt_ref: (bi, kt_ref[st], 0))],
        out_specs=pl.BlockSpec((1, tq, 2 * V_HEAD), q_idx),
        scratch_shapes=[pltpu.VMEM((2, tq, LANES), F32), pltpu.VMEM((2, tq, 2 * V_HEAD), F32)])
    return pl.pallas_call(
        functools.partial(_mla_flash_kernel, c=scale * LOG2E, tq=tq, tk=tk), grid_spec=grid_spec,
        out_shape=jax.ShapeDtypeStruct((b, s, hp * 2 * V_HEAD), BF16),
        compiler_params=_params("parallel", "parallel", "arbitrary"),
    )(qt, kt, qn, qp, kvb, kpe2)


def _diff_finish(acc_ref, l_ref, lam_ref, g_ref):
    o = acc_ref[0] / l_ref[0] - lam_ref[...] * (acc_ref[1] / l_ref[1])
    return o * lax.rsqrt(jnp.mean(o * o, axis=-1, keepdims=True) + DIFF_EPS) * g_ref[...]


def _diff_flash_kernel(qt_ref, kt_ref, q_ref, k_ref, v_ref, lam_ref, g_ref, o_ref, m_ref, l_ref, acc_ref,
                       *, c, tq, tk):
    st = pl.program_id(2)
    qi, ki = qt_ref[st], kt_ref[st]

    @pl.when(ki == 0)
    def _():
        m_ref[...] = jnp.full_like(m_ref, -jnp.inf)
        l_ref[...] = jnp.zeros_like(l_ref)
        acc_ref[...] = jnp.zeros_like(acc_ref)

    def update(masked):
        v = v_ref[0]
        for mp in range(2):
            s = _dot_nt(q_ref[0, :, mp * DIFF_HEAD:(mp + 1) * DIFF_HEAD],
                        k_ref[0, :, mp * DIFF_HEAD:(mp + 1) * DIFF_HEAD])
            if masked:
                s = _causal_mask(s, qi, ki, tq, tk)
            m_new, alpha, p = _online_softmax_tiles(s, m_ref[mp], c)
            row_sum = jnp.sum(functools.reduce(jnp.add, p), axis=-1, keepdims=True)
            l_ref[mp] = alpha * l_ref[mp] + row_sum
            pv = jnp.dot(jnp.concatenate(p, axis=-1).astype(BF16), v, preferred_element_type=F32)
            acc_ref[mp] = jnp.concatenate([alpha, alpha], axis=-1) * acc_ref[mp] + pv
            m_ref[mp] = m_new

    @pl.when(ki < qi)
    def _():
        update(False)

    @pl.when(ki == qi)
    def _():
        update(True)
        a0 = acc_ref[0] / jnp.concatenate([l_ref[0], l_ref[0]], axis=-1)
        a1 = acc_ref[1] / jnp.concatenate([l_ref[1], l_ref[1]], axis=-1)
        o = a0 - lam_ref[...] * a1
        o = o * lax.rsqrt(jnp.mean(o * o, axis=-1, keepdims=True) + DIFF_EPS) * g_ref[...]
        o_ref[0] = o.astype(o_ref.dtype)


def diff_flash(q, k, v, lam_row, gain_row, scale, tq=512):
    b, s, hd = q.shape
    w = 2 * DIFF_HEAD
    nh = hd // w
    tq = min(tq, s)
    tk = tq
    qt, kt = _causal_pairs(s // tq)
    q_idx = lambda bi, h, st, qt_ref, kt_ref: (bi, qt_ref[st], h)
    kv_idx = lambda bi, h, st, qt_ref, kt_ref: (bi, kt_ref[st], 0)
    vec = pl.BlockSpec((1, w), lambda bi, h, st, qt_ref, kt_ref: (0, 0))
    grid_spec = pltpu.PrefetchScalarGridSpec(
        num_scalar_prefetch=2, grid=(b, nh, qt.shape[0]),
        in_specs=[pl.BlockSpec((1, tq, w), q_idx), pl.BlockSpec((1, tk, w), kv_idx),
                  pl.BlockSpec((1, tk, w), kv_idx), vec, vec],
        out_specs=pl.BlockSpec((1, tq, w), q_idx),
        scratch_shapes=[pltpu.VMEM((2, tq, LANES), F32), pltpu.VMEM((2, tq, LANES), F32),
                        pltpu.VMEM((2, tq, w), F32)])
    return pl.pallas_call(
        functools.partial(_diff_flash_kernel, c=scale * LOG2E, tq=tq, tk=tk), grid_spec=grid_spec,
        out_shape=jax.ShapeDtypeStruct((b, s, hd), BF16),
        compiler_params=_params("parallel", "parallel", "arbitrary"),
    )(qt, kt, q, k, v, lam_row, gain_row)


def _mla_decode_kernel(pt_ref, ql_ref, qp_ref, cn_ref, kn_ref, *refs, scale, pg):
    ckv_refs, kpe_refs = refs[:pg], refs[pg:2 * pg]
    o_ref, m_ref, l_ref, acc_ref = refs[2 * pg:]
    c = pl.program_id(1)
    ql, qp = ql_ref[0], qp_ref[0]

    @pl.when(c == 0)
    def _():
        cn = cn_ref[0].astype(BF16).astype(F32)
        kn = kn_ref[0].astype(BF16).astype(F32)
        s_new = (jnp.sum(ql.astype(F32) * cn, axis=-1, keepdims=True)
                 + jnp.sum(qp.astype(F32) * kn, axis=-1, keepdims=True)) * scale
        m_ref[...] = s_new
        l_ref[...] = jnp.ones_like(l_ref)
        acc_ref[...] = jnp.broadcast_to(cn, acc_ref.shape)

    pages = [r[0].astype(BF16) for r in ckv_refs]
    s = jnp.concatenate(
        [_dot_nt(ql, pages[g])
         + jnp.dot(qp, kpe_refs[g][0].astype(BF16), preferred_element_type=F32) for g in range(pg)],
        axis=-1) * scale
    m_prev = m_ref[...]
    m_new = jnp.maximum(m_prev, jnp.max(s, axis=-1, keepdims=True))
    alpha = jnp.exp(m_prev - m_new)
    p = jnp.exp(s - m_new)
    l_ref[...] = alpha * l_ref[...] + jnp.sum(p, axis=-1, keepdims=True)
    pb = p.astype(BF16)
    acc = alpha * acc_ref[...]
    for g in range(pg):
        acc = acc + jnp.dot(pb[:, g * PAGE:(g + 1) * PAGE], pages[g], preferred_element_type=F32)
    acc_ref[...] = acc
    m_ref[...] = m_new

    @pl.when(c == pl.num_programs(1) - 1)
    def _():
        o_ref[0] = (acc_ref[...] / l_ref[...]).astype(o_ref.dtype)


def mla_decode(page_table, q_lat, q_pe, ckv_new, kpe_new, cache_ckv, cache_kpe_t, scale, pg=32):
    b, h, c_dim = q_lat.shape
    r_dim = q_pe.shape[2]
    n_pages = page_table.shape[1]
    pg = min(pg, n_pages)
    pt = page_table.reshape(-1)

    def page_idx(g):
        return lambda bi, ci, pt_ref: (pt_ref[bi * n_pages + ci * pg + g], 0, 0)

    per_b = lambda bi, ci, pt_ref: (bi, 0, 0)
    in_specs = [pl.BlockSpec((1, h, c_dim), per_b), pl.BlockSpec((1, h, r_dim), per_b),
                pl.BlockSpec((1, 1, c_dim), per_b), pl.BlockSpec((1, 1, r_dim), per_b)]
    in_specs += [pl.BlockSpec((1, PAGE, c_dim), page_idx(g)) for g in range(pg)]
    in_specs += [pl.BlockSpec((1, r_dim, PAGE), page_idx(g)) for g in range(pg)]
    grid_spec = pltpu.PrefetchScalarGridSpec(
        num_scalar_prefetch=1, grid=(b, n_pages // pg), in_specs=in_specs,
        out_specs=pl.BlockSpec((1, h, c_dim), per_b),
        scratch_shapes=[pltpu.VMEM((h, 1), F32), pltpu.VMEM((h, 1), F32), pltpu.VMEM((h, c_dim), F32)])
    return pl.pallas_call(
        functools.partial(_mla_decode_kernel, scale=scale, pg=pg), grid_spec=grid_spec,
        out_shape=jax.ShapeDtypeStruct((b, h, c_dim), BF16),
        compiler_params=_params("parallel", "arbitrary"),
    )(pt, q_lat, q_pe, ckv_new, kpe_new, *([cache_ckv] * pg), *([cache_kpe_t] * pg))


def _diff_decode_kernel(pt_ref, q_ref, kn_ref, vn_ref, lam_ref, g_ref, *refs, scale, pg):
    k_refs, v_refs = refs[:pg], refs[pg:2 * pg]
    o_ref, m_ref, l_ref, acc_ref = refs[2 * pg:]
    c = pl.program_id(1)

    @pl.when(c == 0)
    def _():
        kn = kn_ref[0].astype(BF16).astype(F32)
        vn = vn_ref[0].astype(BF16).astype(F32)
        for mp in range(2):
            qm = q_ref[0, mp].astype(F32)
            km = kn[:, mp * DIFF_HEAD:(mp + 1) * DIFF_HEAD]
            m_ref[mp] = jnp.sum(qm * km, axis=-1, keepdims=True) * scale
            acc_ref[mp] = jnp.broadcast_to(vn, acc_ref.shape[1:])
        l_ref[...] = jnp.ones_like(l_ref)

    vs = [r[0].astype(BF16) for r in v_refs]
    for mp in range(2):
        qm = q_ref[0, mp]
        s = jnp.concatenate(
            [_dot_nt(qm, k_refs[g][0, pl.ds(mp, PAGE, stride=2), :].astype(BF16)) for g in range(pg)],
            axis=-1) * scale
        m_prev = m_ref[mp]
        m_new = jnp.maximum(m_prev, jnp.max(s, axis=-1, keepdims=True))
        alpha = jnp.exp(m_prev - m_new)
        p = jnp.exp(s - m_new)
        l_ref[mp] = alpha * l_ref[mp] + jnp.sum(p, axis=-1, keepdims=True)
        pb = p.astype(BF16)
        acc = alpha * acc_ref[mp]
        for g in range(pg):
            acc = acc + jnp.dot(pb[:, g * PAGE:(g + 1) * PAGE], vs[g], preferred_element_type=F32)
        acc_ref[mp] = acc
        m_ref[mp] = m_new

    @pl.when(c == pl.num_programs(1) - 1)
    def _():
        o_ref[0] = _diff_finish(acc_ref, l_ref, lam_ref, g_ref).astype(o_ref.dtype)


def diff_decode(page_table, q, k_new, v_new, lam_row, gain_row, cache_k, cache_v, scale, pg=32):
    b, _, h, _ = q.shape
    n_pages = page_table.shape[1]
    pg = min(pg, n_pages)
    pt = page_table.reshape(-1)
    w = 2 * DIFF_HEAD

    def page_idx(g):
        return lambda bi, ci, pt_ref: (pt_ref[bi * n_pages + ci * pg + g], 0, 0)

    per_b3 = lambda bi, ci, pt_ref: (bi, 0, 0)
    vec = pl.BlockSpec((1, w), lambda bi, ci, pt_ref: (0, 0))
    in_specs = [pl.BlockSpec((1, 2, h, DIFF_HEAD), lambda bi, ci, pt_ref: (bi, 0, 0, 0)),
                pl.BlockSpec((1, 1, w), per_b3), pl.BlockSpec((1, 1, w), per_b3), vec, vec]
    in_specs += [pl.BlockSpec((1, 2 * PAGE, DIFF_HEAD), page_idx(g)) for g in range(pg)]
    in_specs += [pl.BlockSpec((1, PAGE, w), page_idx(g)) for g in range(pg)]
    grid_spec = pltpu.PrefetchScalarGridSpec(
        num_scalar_prefetch=1, grid=(b, n_pages // pg), in_specs=in_specs,
        out_specs=pl.BlockSpec((1, h, w), per_b3),
        scratch_shapes=[pltpu.VMEM((2, h, 1), F32), pltpu.VMEM((2, h, 1), F32), pltpu.VMEM((2, h, w), F32)])
    return pl.pallas_call(
        functools.partial(_diff_decode_kernel, scale=scale, pg=pg), grid_spec=grid_spec,
        out_shape=jax.ShapeDtypeStruct((b, h, w), BF16),
        compiler_params=_params("parallel", "arbitrary"),
    )(pt, q, k_new, v_new, lam_row, gain_row, *([cache_k] * pg), *([cache_v] * pg))


def _head_nt_kernel(x_ref, w_ref, o_ref):
    o_ref[...] = _dot_nt(x_ref[...].astype(BF16), w_ref[...].astype(BF16)).astype(o_ref.dtype)


def _head_nn_kernel(x_ref, w_ref, o_ref):
    o_ref[...] = jnp.dot(x_ref[...].astype(BF16), w_ref[...].astype(BF16),
                         preferred_element_type=F32).astype(o_ref.dtype)


def absorb_q(q_nope, w_ukv2):
    b, hd = q_nope.shape
    c_dim = w_ukv2.shape[0]
    nh = hd // QK_NOPE
    return pl.pallas_call(
        _head_nt_kernel, grid=(nh,),
        in_specs=[pl.BlockSpec((b, QK_NOPE), lambda h: (0, h)),
                  pl.BlockSpec((c_dim, QK_NOPE), lambda h: (0, 2 * h))],
        out_specs=pl.BlockSpec((b, c_dim), lambda h: (0, h)),
        out_shape=jax.ShapeDtypeStruct((b, nh * c_dim), BF16),
        compiler_params=_params("parallel"),
    )(q_nope, w_ukv2)


def expand_o(o_lat, w_ukv2):
    b = o_lat.shape[0]
    c_dim = w_ukv2.shape[0]
    nh = o_lat.shape[1] // c_dim
    return pl.pallas_call(
        _head_nn_kernel, grid=(nh,),
        in_specs=[pl.BlockSpec((b, c_dim), lambda h: (0, h)),
                  pl.BlockSpec((c_dim, V_HEAD), lambda h: (0, 2 * h + 1))],
        out_specs=pl.BlockSpec((b, V_HEAD), lambda h: (0, h)),
        out_shape=jax.ShapeDtypeStruct((b, nh * V_HEAD), BF16),
        compiler_params=_params("parallel"),
    )(o_lat, w_ukv2)


def _wkv_kernel(r_ref, w_ref, k_ref, v_ref, a_ref, b_ref, s0_ref, y_ref, st_ref, s_ref, *, ts, n):
    c = pl.program_id(1)

    @pl.when(c == 0)
    def _():
        s_ref[...] = s0_ref[...]

    def row(ref, t, j):
        return ref[t, pl.ds(j, 1), :]

    sa0 = jnp.zeros((n, LANES), F32)
    for j in range(n):
        sa0 = sa0 + s_ref[j] * row(a_ref, 0, j)

    def step(t, sa):
        t_next = jnp.minimum(t + 1, ts - 1)
        v = v_ref[t]
        y = jnp.zeros((n, LANES), F32)
        sa_next = jnp.zeros((n, LANES), F32)
        for j in range(n):
            sj = s_ref[j] * row(w_ref, t, j) + sa * row(b_ref, t, j) + v * row(k_ref, t, j)
            s_ref[j] = sj
            y = y + sj * row(r_ref, t, j)
            sa_next = sa_next + sj * row(a_ref, t_next, j)
        y_ref[t] = y
        return sa_next

    lax.fori_loop(0, ts, step, sa0)

    @pl.when(c == pl.num_programs(1) - 1)
    def _():
        st_ref[...] = s_ref[...]


def wkv_scan(r, w, k, v, a, b, s0, ts=32):
    s, n, l = r.shape
    ts = min(ts, s)
    seq_spec = pl.BlockSpec((ts, n, LANES), lambda g, c: (c, 0, g))
    st_spec = pl.BlockSpec((n, n, LANES), lambda g, c: (0, 0, g))
    return pl.pallas_call(
        functools.partial(_wkv_kernel, ts=ts, n=n),
        grid=(l // LANES, s // ts),
        in_specs=[seq_spec] * 6 + [st_spec],
        out_specs=[seq_spec, st_spec],
        out_shape=[jax.ShapeDtypeStruct((s, n, l), F32), jax.ShapeDtypeStruct((n, n, l), F32)],
        scratch_shapes=[pltpu.VMEM((n, n, LANES), F32)],
        compiler_params=_params("parallel", "arbitrary"),
    )(r, w, k, v, a, b, s0)


def _wkv_fused_kernel(r_ref, k_ref, v_ref, lw_ref, a_ref, kk_ref, ka_ref, rk_ref, g_ref, b_ref, s0_ref,
                      z_ref, st_ref, s_ref, w_s, k_s, a_s, b_s, y_s, *, ts, n):
    c = pl.program_id(1)

    @pl.when(c == 0)
    def _():
        s_ref[...] = s0_ref[...]

    def prep(t, carry):
        k, a = k_ref[t], a_ref[t]
        kk = k * kk_ref[...]
        kk = kk / jnp.maximum(jnp.sqrt(jnp.sum(kk * kk, axis=0, keepdims=True)), 1e-12)
        w_s[t] = jnp.exp(-jnp.exp(-jax.nn.softplus(-lw_ref[t]) - 0.5))
        k_s[t] = k * (1.0 + (a - 1.0) * ka_ref[...])
        a_s[t] = -kk
        b_s[t] = kk * a
        return carry

    lax.fori_loop(0, ts, prep, 0)

    def row(ref, t, j):
        return ref[t, pl.ds(j, 1), :]

    sa0 = jnp.zeros((n, LANES), F32)
    for j in range(n):
        sa0 = sa0 + s_ref[j] * row(a_s, 0, j)

    def step(t, sa):
        t_next = jnp.minimum(t + 1, ts - 1)
        v = v_ref[t]
        y = jnp.zeros((n, LANES), F32)
        sa_next = jnp.zeros((n, LANES), F32)
        for j in range(n):
            sj = s_ref[j] * row(w_s, t, j) + sa * row(b_s, t, j) + v * row(k_s, t, j)
            s_ref[j] = sj
            y = y + sj * row(r_ref, t, j)
            sa_next = sa_next + sj * row(a_s, t_next, j)
        y_s[t] = y
        return sa_next

    lax.fori_loop(0, ts, step, sa0)

    def post(t, carry):
        y = y_s[t]
        yc = y - jnp.mean(y, axis=0, keepdims=True)
        var = jnp.mean(yc * yc, axis=0, keepdims=True)
        yn = yc * lax.rsqrt(var + RWKV_GN_EPS) * g_ref[...] + b_ref[...]
        bonus = jnp.sum(r_ref[t] * k_s[t] * rk_ref[...], axis=0, keepdims=True) * v_ref[t]
        z_ref[t] = yn + bonus
        return carry

    lax.fori_loop(0, ts, post, 0)

    @pl.when(c == pl.num_programs(1) - 1)
    def _():
        st_ref[...] = s_ref[...]


def wkv_scan_fused(seqs, planes, s0, ts=32):
    s, n, l = seqs[0].shape
    ts = min(ts, s)
    seq_spec = pl.BlockSpec((ts, n, LANES), lambda g, c: (c, 0, g))
    plane_spec = pl.BlockSpec((n, LANES), lambda g, c: (0, g))
    st_spec = pl.BlockSpec((n, n, LANES), lambda g, c: (0, 0, g))
    return pl.pallas_call(
        functools.partial(_wkv_fused_kernel, ts=ts, n=n),
        grid=(l // LANES, s // ts),
        in_specs=[seq_spec] * 5 + [plane_spec] * 5 + [st_spec],
        out_specs=[seq_spec, st_spec],
        out_shape=[jax.ShapeDtypeStruct((s, n, l), F32), jax.ShapeDtypeStruct((n, n, l), F32)],
        scratch_shapes=[pltpu.VMEM((n, n, LANES), F32)] + [pltpu.VMEM((ts, n, LANES), F32)] * 5,
        compiler_params=_params("parallel", "arbitrary"),
    )(*seqs, *planes, s0)


def _rope_tables(pos, half):
    inv = ROPE_THETA ** (-jnp.arange(half, dtype=F32) / half)
    ang = pos.astype(F32)[:, None] * inv[None, :]
    return jnp.cos(ang), jnp.sin(ang)


def _rope(x, cos, sin, n_rot):
    half = n_rot // 2
    shape = (x.shape[0],) + (1,) * (x.ndim - 2) + (half,)
    cos, sin = cos.reshape(shape), sin.reshape(shape)
    x1, x2 = x[..., :half], x[..., half:n_rot]
    return jnp.concatenate([x1 * cos - x2 * sin, x2 * cos + x1 * sin, x[..., n_rot:]], axis=-1)


def _rms(x, g, eps):
    return x * lax.rsqrt(jnp.mean(jnp.square(x), axis=-1, keepdims=True) + eps) * g


class _Tokens:
    def __init__(self, batch, seq, dec_batch, past_len):
        self.batch, self.seq, self.dec_batch, self.past_len = batch, seq, dec_batch, past_len
        self.n_prompt = batch * seq
        self.n_tokens = self.n_prompt + dec_batch
        self.pos = jnp.concatenate([jnp.tile(jnp.arange(seq, dtype=jnp.int32), batch),
                                    jnp.full((dec_batch,), past_len, jnp.int32)])


def _mla_mixer(tk, h, p, cache_ckv, cache_kpe, page_table):
    n_p, nb, seq, ns = tk.n_prompt, tk.batch, tk.seq, tk.dec_batch
    c_dim = p['w_dkv'].shape[1] - QK_ROPE
    nh = p['w_ukv'].shape[1]
    qd = QK_NOPE + QK_ROPE
    scale = qd ** -0.5
    cos, sin = _rope_tables(tk.pos, QK_ROPE // 2)

    cq = _rms(matmul(h, p['w_dq']), p['q_norm'], MLA_EPS).astype(BF16)
    w_uq = p['w_uq'].reshape(-1, nh, qd)
    w_uq = jnp.concatenate([w_uq[..., :QK_NOPE].reshape(-1, nh * QK_NOPE),
                            w_uq[..., QK_NOPE:].reshape(-1, nh * QK_ROPE)], axis=1)
    q = matmul(cq, w_uq)
    q_nope = q[:, :nh * QK_NOPE].astype(BF16)
    q_pe = _rope(q[:, nh * QK_NOPE:].reshape(-1, nh, QK_ROPE), cos, sin, QK_ROPE).astype(BF16)
    q_pe = q_pe.reshape(-1, nh * QK_ROPE)

    kv = matmul(h, p['w_dkv'])
    ckv = _rms(kv[:, :c_dim], p['kv_norm'], MLA_EPS)
    kpe = _rope(kv[:, c_dim:], cos, sin, QK_ROPE)
    w_ukv2 = p['w_ukv'].reshape(c_dim, nh * (QK_NOPE + V_HEAD))

    kvb = matmul(ckv[:n_p].astype(BF16), w_ukv2, out_dtype=BF16)
    kpe_b = kpe[:n_p].astype(BF16)
    zeros = jnp.zeros_like(kpe_b)
    kpe2 = jnp.concatenate([kpe_b, zeros, zeros, kpe_b], axis=-1)
    o_p = mla_flash(q_nope[:n_p].reshape(nb, seq, -1), q_pe[:n_p].reshape(nb, seq, -1),
                    kvb.reshape(nb, seq, -1), kpe2.reshape(nb, seq, -1), scale)

    q_lat = absorb_q(q_nope[n_p:], w_ukv2).reshape(ns, nh, c_dim)
    o_lat = mla_decode(page_table, q_lat, q_pe[n_p:].reshape(ns, nh, QK_ROPE),
                       ckv[n_p:].reshape(ns, 1, c_dim), kpe[n_p:].reshape(ns, 1, QK_ROPE),
                       cache_ckv, jnp.swapaxes(cache_kpe, 1, 2), scale)
    o_s = expand_o(o_lat.reshape(ns, nh * c_dim), w_ukv2)

    o = jnp.concatenate([o_p.reshape(n_p, -1), o_s], axis=0)
    out = matmul(o, p['w_o'])
    return out, (ckv[:n_p].reshape(nb, seq, c_dim), kpe[:n_p].reshape(nb, seq, QK_ROPE),
                 ckv[n_p:].reshape(ns, 1, c_dim), kpe[n_p:].reshape(ns, 1, QK_ROPE))


def _diff_mixer(tk, h, p, cache_k, cache_v, page_table, layer_idx):
    n_p, nb, seq, ns = tk.n_prompt, tk.batch, tk.seq, tk.dec_batch
    w = 2 * DIFF_HEAD
    nh = p['w_q'].shape[1] // w
    scale = DIFF_HEAD ** -0.5
    cos, sin = _rope_tables(tk.pos, DIFF_ROT // 2)
    lam_init = 0.8 - 0.6 * math.exp(-0.3 * layer_idx)
    lam = (jnp.exp(jnp.sum(p['lam_q1'] * p['lam_k1'])) - jnp.exp(jnp.sum(p['lam_q2'] * p['lam_k2'])) + lam_init)
    lam_row = jnp.full((1, w), lam, F32)
    gain_row = (p['subln'] * (1.0 - lam_init)).reshape(1, w)

    q = _rope(matmul(h, p['w_q']).reshape(-1, nh, 2, DIFF_HEAD), cos, sin, DIFF_ROT).astype(BF16)
    k = _rope(matmul(h, p['w_k']).reshape(-1, 2, DIFF_HEAD), cos, sin, DIFF_ROT)
    v = matmul(h, p['w_v'])
    k2 = k.reshape(-1, w)

    o_p = diff_flash(q[:n_p].reshape(nb, seq, nh * w), k2[:n_p].astype(BF16).reshape(nb, seq, w),
                     v[:n_p].astype(BF16).reshape(nb, seq, w), lam_row, gain_row, scale)
    q_s = jnp.swapaxes(q[n_p:], 1, 2)
    o_s = diff_decode(page_table, q_s, k2[n_p:].reshape(ns, 1, w), v[n_p:].reshape(ns, 1, w),
                      lam_row, gain_row, cache_k.reshape(-1, 2 * PAGE, DIFF_HEAD), cache_v, scale)
    o = jnp.concatenate([o_p.reshape(n_p, -1), o_s.reshape(ns, -1)], axis=0)
    out = matmul(o, p['w_o'])
    return out, (k[:n_p].reshape(nb, seq, 2, DIFF_HEAD), v[:n_p].reshape(nb, seq, w),
                 k[n_p:].reshape(ns, 1, 2, DIFF_HEAD), v[n_p:].reshape(ns, 1, w))


def _rwkv_mixer(tk, h, p, shift0, wkv0):
    n_p, nb, seq, ns = tk.n_prompt, tk.batch, tk.seq, tk.dec_batch
    d = h.shape[1]
    n = RWKV_HEAD
    nh = d // n
    hp = h[:n_p].reshape(nb, seq, d)
    prev = jnp.concatenate([jnp.zeros((nb, 1, d), F32), hp[:, :-1]], axis=1).reshape(n_p, d)
    prev = jnp.concatenate([prev, shift0], axis=0)
    xx = prev - h
    xr, xw, xk, xv, xa, xg = ((h + xx * p['mix'][j]).astype(BF16) for j in range(6))
    r = matmul(xr, p['w_r'])
    k = matmul(xk, p['w_k'])
    v = matmul(xv, p['w_v'])
    lw = matmul(matmul(xw, p['w1'], act="tanh", out_dtype=BF16), p['w2'], bias=p['w0'])
    a = matmul(matmul(xa, p['a1'], out_dtype=BF16), p['a2'], bias=p['a0'], act="sigmoid")
    g = matmul(matmul(xg, p['g1'], act="sigmoid", out_dtype=BF16), p['g2'])

    def planes(reps):
        return tuple(jnp.tile(p[name].reshape(nh, n).T, (1, reps)) for name in ('k_k', 'k_a', 'r_k', 'lnx_g', 'lnx_b'))

    def lanes_p(t):
        return t[:n_p].reshape(nb, seq, nh, n).transpose(1, 3, 0, 2).reshape(seq, n, nb * nh)

    def lanes_s(t):
        return t[n_p:].reshape(ns, nh, n).transpose(2, 0, 1).reshape(1, n, ns * nh)

    seqs = (r, k, v, lw, a)
    y_p, st_p = wkv_scan_fused(tuple(lanes_p(t) for t in seqs), planes(nb), jnp.zeros((n, n, nb * nh), F32))
    s0_s = wkv0.transpose(3, 2, 0, 1).reshape(n, n, ns * nh)
    y_s, st_s = wkv_scan_fused(tuple(lanes_s(t) for t in seqs), planes(ns), s0_s)
    y = jnp.concatenate([y_p.reshape(seq, n, nb, nh).transpose(2, 0, 3, 1).reshape(n_p, nh, n),
                         y_s.reshape(n, ns, nh).transpose(1, 2, 0)], axis=0)
    wkv_p = st_p.reshape(n, n, nb, nh).transpose(2, 3, 1, 0)
    wkv_s = st_s.reshape(n, n, ns, nh).transpose(2, 3, 1, 0)

    yo = (y.reshape(-1, d) * g).astype(BF16)
    out = matmul(yo, p['w_o'])
    return out, (hp[:, -1], wkv_p, h[n_p:], wkv_s)


def _dense_ffn(h, w1, w3, w2):
    return matmul(swiglu_up(h, w1, w3), w2.astype(BF16), tm_cap=416, split_k=False)


def _moe_ffn(h, router, w1, w3, w2, tm=512):
    t, d = h.shape
    n_exp = router.shape[1]
    logits = matmul(h, router)
    top_v, top_i = lax.top_k(logits, TOP_K)
    gates = jax.nn.softmax(top_v, axis=-1)

    n_assign = t * TOP_K
    n_tiles = pl.cdiv(n_assign, tm) + n_exp
    n_slots = n_tiles * tm
    flat_e = top_i.reshape(-1).astype(jnp.int32)
    order = jnp.argsort(flat_e, stable=True).astype(jnp.int32)
    sorted_e = flat_e[order]
    counts = jnp.zeros((n_exp,), jnp.int32).at[flat_e].add(1)
    padded = ((counts + tm - 1) // tm) * tm
    pad_end = jnp.cumsum(padded)
    pad_start = pad_end - padded
    start = jnp.cumsum(counts) - counts
    dest = pad_start[sorted_e] + (jnp.arange(n_assign, dtype=jnp.int32) - start[sorted_e])
    slot_tok = jnp.zeros((n_slots,), jnp.int32).at[dest].set(order // TOP_K)
    slot_gate = jnp.zeros((n_slots,), F32).at[dest].set(gates.reshape(-1)[order])
    slot_of = jnp.zeros((n_assign,), jnp.int32).at[order].set(dest)
    tile_start = jnp.arange(n_tiles, dtype=jnp.int32) * tm
    tile_expert = jnp.minimum(jnp.searchsorted(pad_end, tile_start, side='right'), n_exp - 1).astype(jnp.int32)
    n_used = (pad_end[-1:] // tm).astype(jnp.int32)

    xg = gather_rows(slot_tok, h, BF16)
    mid = grouped_swiglu_up(tile_expert, n_used, xg, w1, w3, tm)
    y = grouped_down(tile_expert, n_used, mid, w2, slot_gate.reshape(n_slots, 1), tm)
    return combine_pairs(slot_of, y)


def _forward(x_prompt, x_sample, page_table, c_prompt, c_sample, layers):
    nb, seq, d = x_prompt.shape
    ns = x_sample.shape[0]
    n_pages = page_table.shape[1]
    tk = _Tokens(nb, seq, ns, n_pages * PAGE)
    depth = len(layers)
    alpha = (2 * depth) ** 0.25
    x = jnp.concatenate([x_prompt.reshape(nb * seq, d), x_sample.reshape(ns, d)], axis=0)

    pad = (-(ns + nb)) % 16
    c_all = jax.nn.silu(jnp.concatenate([c_sample, c_prompt, jnp.zeros((pad, d), F32)], axis=0)).astype(BF16)
    mods = []
    for L in layers:
        mod = matmul(c_all, L['ada'][0], bias=L['ada'][1])
        chunks = []
        for j in range(6):
            col = mod[:, j * d:(j + 1) * d]
            chunks.append((col[ns:ns + nb].reshape(nb, 1, d), col[:ns]))
        mods.append(chunks)

    nm = functools.partial(norm_mod, n_prompt=tk.n_prompt, seq=seq)
    outs = []
    h_dtypes = lambda i: (BF16, F32) if layers[i]['kind'] == 'rwkv7' else (BF16,)
    hs = nm(x, scale=mods[0][1], shift=mods[0][0], h_dtypes=h_dtypes(0))
    for i, L in enumerate(layers):
        m = mods[i]
        kind = L['kind']
        if kind == 'mla':
            o, new = _mla_mixer(tk, hs[0], L['mix'], L['state'][0], L['state'][1], page_table)
        elif kind == 'diff':
            o, new = _diff_mixer(tk, hs[0], L['mix'], L['state'][0], L['state'][1], page_table, i)
        else:
            o, new = _rwkv_mixer(tk, hs[1], L['mix'], L['state'][0], L['state'][1])
        outs.append(new)
        dense = len(L['ffn']) == 3
        x, h2 = nm(x, sub=o, gate=m[2], ln=L['ln'][:2], scale=m[4], shift=m[3], alpha=alpha,
                   h_dtypes=(BF16,) if dense else (F32,))
        if dense:
            f = _dense_ffn(h2, *L['ffn'])
        else:
            f = _moe_ffn(h2, *L['ffn'])
        if i + 1 < depth:
            res = nm(x, sub=f, gate=m[5], ln=L['ln'][2:], scale=mods[i + 1][1], shift=mods[i + 1][0],
                     h_dtypes=h_dtypes(i + 1), alpha=alpha)
            x, hs = res[0], res[1:]
        else:
            x, = nm(x, sub=f, gate=m[5], ln=L['ln'][2:], alpha=alpha)
    y_prompt = x[:tk.n_prompt].reshape(nb, seq, d)
    y_sample = x[tk.n_prompt:].reshape(ns, 1, d)
    flat = [y_prompt, y_sample]
    for new in outs:
        flat.extend(new)
    return tuple(flat)


def kernel(x_prompt, x_sample, cache_l0_ckv, cache_l0_kpe, cache_l1_k, cache_l1_v, state_l2_shift, state_l2_wkv, cache_l3_ckv, cache_l3_kpe, page_table, c_prompt, c_sample, l0_ada_w, l0_ada_b, l0_ln1_g, l0_ln1_b, l0_ln2_g, l0_ln2_b, l0_w_dq, l0_q_norm, l0_w_uq, l0_w_dkv, l0_kv_norm, l0_w_ukv, l0_w_o, l0_ffn_w1, l0_ffn_w3, l0_ffn_w2, l1_ada_w, l1_ada_b, l1_ln1_g, l1_ln1_b, l1_ln2_g, l1_ln2_b, l1_w_q, l1_w_k, l1_w_v, l1_lam_q1, l1_lam_k1, l1_lam_q2, l1_lam_k2, l1_subln, l1_w_o, l1_router, l1_exp_w1, l1_exp_w3, l1_exp_w2, l2_ada_w, l2_ada_b, l2_ln1_g, l2_ln1_b, l2_ln2_g, l2_ln2_b, l2_mix, l2_w_r, l2_w_k, l2_w_v, l2_w_o, l2_w0, l2_w1, l2_w2, l2_a0, l2_a1, l2_a2, l2_g1, l2_g2, l2_k_k, l2_k_a, l2_r_k, l2_lnx_g, l2_lnx_b, l2_ffn_w1, l2_ffn_w3, l2_ffn_w2, l3_ada_w, l3_ada_b, l3_ln1_g, l3_ln1_b, l3_ln2_g, l3_ln2_b, l3_w_dq, l3_q_norm, l3_w_uq, l3_w_dkv, l3_kv_norm, l3_w_ukv, l3_w_o, l3_router, l3_exp_w1, l3_exp_w3, l3_exp_w2):
    layers = (
        dict(kind='mla', ada=(l0_ada_w, l0_ada_b), ln=(l0_ln1_g, l0_ln1_b, l0_ln2_g, l0_ln2_b),
             mix=dict(w_dq=l0_w_dq, q_norm=l0_q_norm, w_uq=l0_w_uq, w_dkv=l0_w_dkv, kv_norm=l0_kv_norm,
                      w_ukv=l0_w_ukv, w_o=l0_w_o),
             state=(cache_l0_ckv, cache_l0_kpe), ffn=(l0_ffn_w1, l0_ffn_w3, l0_ffn_w2)),
        dict(kind='diff', ada=(l1_ada_w, l1_ada_b), ln=(l1_ln1_g, l1_ln1_b, l1_ln2_g, l1_ln2_b),
             mix=dict(w_q=l1_w_q, w_k=l1_w_k, w_v=l1_w_v, lam_q1=l1_lam_q1, lam_k1=l1_lam_k1,
                      lam_q2=l1_lam_q2, lam_k2=l1_lam_k2, subln=l1_subln, w_o=l1_w_o),
             state=(cache_l1_k, cache_l1_v), ffn=(l1_router, l1_exp_w1, l1_exp_w3, l1_exp_w2)),
        dict(kind='rwkv7', ada=(l2_ada_w, l2_ada_b), ln=(l2_ln1_g, l2_ln1_b, l2_ln2_g, l2_ln2_b),
             mix=dict(mix=l2_mix, w_r=l2_w_r, w_k=l2_w_k, w_v=l2_w_v, w_o=l2_w_o, w0=l2_w0, w1=l2_w1, w2=l2_w2,
                      a0=l2_a0, a1=l2_a1, a2=l2_a2, g1=l2_g1, g2=l2_g2, k_k=l2_k_k, k_a=l2_k_a, r_k=l2_r_k,
                      lnx_g=l2_lnx_g, lnx_b=l2_lnx_b),
             state=(state_l2_shift, state_l2_wkv), ffn=(l2_ffn_w1, l2_ffn_w3, l2_ffn_w2)),
        dict(kind='mla', ada=(l3_ada_w, l3_ada_b), ln=(l3_ln1_g, l3_ln1_b, l3_ln2_g, l3_ln2_b),
             mix=dict(w_dq=l3_w_dq, q_norm=l3_q_norm, w_uq=l3_w_uq, w_dkv=l3_w_dkv, kv_norm=l3_kv_norm,
                      w_ukv=l3_w_ukv, w_o=l3_w_o),
             state=(cache_l3_ckv, cache_l3_kpe), ffn=(l3_router, l3_exp_w1, l3_exp_w3, l3_exp_w2)),
    )
    return _forward(x_prompt, x_sample, page_table, c_prompt, c_sample, layers)
```
